```python
import math
import jax, jax.numpy as jnp
from jax import lax
import numpy as np

D_MODEL = 1024
BATCH = 16
SEQ = 2048
DEPTH = 1

CHUNK = 64
Q_BLOCK = 128
SB_HEADS = 16
SB_HEAD_DIM = 64
SB_WIDTH = SB_HEADS * SB_HEAD_DIM
CV_WIDTH = D_MODEL
CV_KERNEL = 31
LN_EPS = 1e-5
DEEPNORM_ALPHA = (2.0 * DEPTH) ** 0.25
DEEPNORM_BETA = (8.0 * DEPTH) ** -0.25
SPLIT_SIZES = (SB_WIDTH, SB_WIDTH, SB_WIDTH, SB_WIDTH, CV_WIDTH, CV_WIDTH, CV_WIDTH, D_MODEL, D_MODEL)
IN_WIDTH = sum(SPLIT_SIZES)
SPLIT_POINTS = tuple(int(i) for i in np.cumsum(SPLIT_SIZES)[:-1])

kernel_name = "stickbreak_conformer_gated_hybrid"


def _layer_norm(x, g, b):
    xf = x.astype(jnp.float32)
    mu = jnp.mean(xf, axis=-1, keepdims=True)
    var = jnp.mean(jnp.square(xf - mu), axis=-1, keepdims=True)
    y = (xf - mu) * lax.rsqrt(var + LN_EPS) * g.astype(jnp.float32) + b.astype(jnp.float32)
    return y.astype(x.dtype)


def _stick_breaking_attention(q, k, v):
    seq = q.shape[1]
    scale = 1.0 / math.sqrt(q.shape[-1])
    qf = q.astype(jnp.float32) * scale
    kf = k.astype(jnp.float32)
    vf = v.astype(jnp.float32)
    outs = []
    for start in range(0, seq, Q_BLOCK):
        end = start + Q_BLOCK
        logits = jnp.einsum('bqhd,bkhd->bhqk', qf[:, start:end], kf[:, :end])
        t_idx = start + jnp.arange(Q_BLOCK)[:, None]
        s_idx = jnp.arange(end)[None, :]
        mask = s_idx < t_idx
        log_not_beta = jnp.where(mask, jax.nn.log_sigmoid(-logits), 0.0)
        later = lax.cumsum(log_not_beta, axis=3, reverse=True) - log_not_beta
        weights = jnp.where(mask, jnp.exp(jax.nn.log_sigmoid(logits) + later), 0.0)
        outs.append(jnp.einsum('bhqk,bkhd->bqhd', weights, vf[:, :end]))
    return jnp.concatenate(outs, axis=1).astype(q.dtype)


def _causal_depthwise_conv(u, w, b):
    kw = w.shape[0]
    y = lax.conv_general_dilated(
        u, w[:, None, :].astype(u.dtype), window_strides=(1,), padding=[(kw - 1, 0)],
        dimension_numbers=('NWC', 'WIO', 'NWC'), feature_group_count=u.shape[-1])
    return y + b.astype(u.dtype)


def _hybrid_layer(x, w_in, w_sb_proj, conv_w, conv_b, conv_ln_g, conv_ln_b,
                  w_cv_proj, w_out, ln_g, ln_b):
    bsz, seq, _ = x.shape
    h = x @ w_in
    q, k, v, z_sb, c_val, c_gate, z_cv, g_sb, g_cv = jnp.split(h, SPLIT_POINTS, axis=-1)

    to_heads = lambda t: t.reshape(bsz, seq, SB_HEADS, SB_HEAD_DIM)
    o_sb = _stick_breaking_attention(to_heads(q), to_heads(k), to_heads(v)).reshape(bsz, seq, SB_WIDTH)
    y_sb = (o_sb * jax.nn.silu(z_sb)) @ w_sb_proj

    u = c_val * jax.nn.sigmoid(c_gate)
    u = _causal_depthwise_conv(u, conv_w, conv_b)
    u = jax.nn.silu(_layer_norm(u, conv_ln_g, conv_ln_b))
    y_cv = (u * jax.nn.silu(z_cv)) @ w_cv_proj

    merged = jax.nn.sigmoid(g_sb) * y_sb + jax.nn.sigmoid(g_cv) * y_cv
    return _layer_norm(DEEPNORM_ALPHA * x + merged @ w_out, ln_g, ln_b)


def setup_inputs(seed: int = 0) -> dict:
    key = jax.random.key(seed)
    ks = jax.random.split(key, 14)
    n = lambda kk, shape: jax.random.normal(kk, shape, dtype=jnp.float32)
    x = n(ks[0], (BATCH, SEQ, D_MODEL))
    ln_in_g = 1.0 + 0.02 * n(ks[1], (D_MODEL,))
    ln_in_b = 0.02 * n(ks[2], (D_MODEL,))
    w_in = n(ks[3], (DEPTH, D_MODEL, IN_WIDTH)) * D_MODEL ** -0.5
    w_sb_proj = n(ks[4], (DEPTH, SB_WIDTH, D_MODEL)) * (SB_WIDTH ** -0.5) * DEEPNORM_BETA
    conv_w = n(ks[5], (DEPTH, CV_KERNEL, CV_WIDTH)) * CV_KERNEL ** -0.5
    conv_b = 0.02 * n(ks[6], (DEPTH, CV_WIDTH))
    conv_ln_g = 1.0 + 0.02 * n(ks[7], (DEPTH, CV_WIDTH))
    conv_ln_b = 0.02 * n(ks[8], (DEPTH, CV_WIDTH))
    w_cv_proj = n(ks[9], (DEPTH, CV_WIDTH, D_MODEL)) * (CV_WIDTH ** -0.5) * DEEPNORM_BETA
    w_out = n(ks[10], (DEPTH, D_MODEL, D_MODEL)) * (D_MODEL ** -0.5) * DEEPNORM_BETA
    ln_post_g = 1.0 + 0.02 * n(ks[11], (DEPTH, D_MODEL))
    ln_post_b = 0.02 * n(ks[12], (DEPTH, D_MODEL))
    return {"x": x, "ln_in_g": ln_in_g, "ln_in_b": ln_in_b, "w_in": w_in,
            "w_sb_proj": w_sb_proj, "conv_w": conv_w, "conv_b": conv_b,
            "conv_ln_g": conv_ln_g, "conv_ln_b": conv_ln_b, "w_cv_proj": w_cv_proj,
            "w_out": w_out, "ln_post_g": ln_post_g, "ln_post_b": ln_post_b}


def reference(x, ln_in_g, ln_in_b, w_in, w_sb_proj, conv_w, conv_b, conv_ln_g, conv_ln_b,
              w_cv_proj, w_out, ln_post_g, ln_post_b):
    h = _layer_norm(x, ln_in_g, ln_in_b)
    for l in range(DEPTH):
        h = _hybrid_layer(h, w_in[l], w_sb_proj[l], conv_w[l], conv_b[l], conv_ln_g[l],
                          conv_ln_b[l], w_cv_proj[l], w_out[l], ln_post_g[l], ln_post_b[l])
    return h
```

```python
import functools
import math

import jax
import jax.numpy as jnp
from jax import lax
from jax.experimental import pallas as pl
from jax.experimental.pallas import tpu as pltpu

D_MODEL = 1024
SB_HEADS = 16
SB_HEAD_DIM = 64
CV_KERNEL = 31
LN_EPS = 1e-5
DEPTH = 1
DEEPNORM_ALPHA = (2.0 * DEPTH) ** 0.25
N_SPLITS = 9

LOG2E = 1.4426950408889634
Q_SCALE = LOG2E / math.sqrt(SB_HEAD_DIM)

V7X_VMEM_LIMIT_BYTES = 56 * 1024 * 1024
LANES = 128
SUBLANES = 8

PROJ_ROWS = 512
ATT_TILE = 256
OUT_ROWS = 256
CONV_HALO = 32
CONV_STRIDE = 4
CONV_CHUNK = CONV_STRIDE * SUBLANES


def _layer_norm_f32(x, g, b):
    mu = jnp.mean(x, axis=-1, keepdims=True)
    xc = x - mu
    var = jnp.mean(xc * xc, axis=-1, keepdims=True)
    return xc * lax.rsqrt(var + LN_EPS) * g + b


def _sigmoid(x):
    return 1.0 / (1.0 + jnp.exp(-x))


def _in_proj_kernel(x_ref, g_ref, b_ref, w_ref,
                    q_ref, k_ref, v_ref, zs_ref, u_ref, zc_ref, gs_ref, gc_ref,
                    hn_ref, cval_ref):
    j = pl.program_id(1)

    @pl.when(j == 0)
    def _():
        hn_ref[...] = _layer_norm_f32(x_ref[...], g_ref[...], b_ref[...]).astype(jnp.bfloat16)

    acc = jnp.dot(hn_ref[...], w_ref[...], preferred_element_type=jnp.float32)

    @pl.when(j == 0)
    def _():
        q_ref[...] = (acc * Q_SCALE).astype(q_ref.dtype)

    @pl.when(j == 1)
    def _():
        k_ref[...] = acc.astype(k_ref.dtype)

    @pl.when(j == 2)
    def _():
        v_ref[...] = acc.astype(v_ref.dtype)

    @pl.when(j == 3)
    def _():
        zs_ref[...] = (acc * _sigmoid(acc)).astype(zs_ref.dtype)

    @pl.when(j == 4)
    def _():
        cval_ref[...] = acc

    @pl.when(j == 5)
    def _():
        u_ref[...] = (cval_ref[...] * _sigmoid(acc)).astype(u_ref.dtype)

    @pl.when(j == 6)
    def _():
        zc_ref[...] = (acc * _sigmoid(acc)).astype(zc_ref.dtype)

    @pl.when(j == 7)
    def _():
        gs_ref[...] = _sigmoid(acc).astype(gs_ref.dtype)

    @pl.when(j == 8)
    def _():
        gc_ref[...] = _sigmoid(acc).astype(gc_ref.dtype)


def _in_proj(x2d, ln_g, ln_b, w_in_bf16):
    n_rows = x2d.shape[0]
    tm = PROJ_ROWS
    row_spec = pl.BlockSpec((tm, D_MODEL), lambda i, j: (i, 0))
    vec_spec = pl.BlockSpec((1, D_MODEL), lambda i, j: (0, 0))
    bf = jax.ShapeDtypeStruct((n_rows, D_MODEL), jnp.bfloat16)
    f32 = jax.ShapeDtypeStruct((n_rows, D_MODEL), jnp.float32)
    return pl.pallas_call(
        _in_proj_kernel,
        grid=(n_rows // tm, N_SPLITS),
        in_specs=[row_spec, vec_spec, vec_spec,
                  pl.BlockSpec((D_MODEL, D_MODEL), lambda i, j: (0, j))],
        out_specs=[row_spec] * 8,
        out_shape=[bf, bf, bf, bf, f32, bf, bf, bf],
        scratch_shapes=[pltpu.VMEM((tm, D_MODEL), jnp.bfloat16),
                        pltpu.VMEM((tm, D_MODEL), jnp.float32)],
        compiler_params=pltpu.CompilerParams(
            dimension_semantics=("arbitrary", "arbitrary"),
            vmem_limit_bytes=V7X_VMEM_LIMIT_BYTES),
        name="in_proj",
    )(x2d, ln_g, ln_b, w_in_bf16)


def _softplus2(l):
    return jnp.maximum(l, 0.0) + jnp.log2(1.0 + jnp.exp2(-jnp.abs(l)))


def _sb_tile(l, v_tile, suffix, rsum, mask):
    sp = _softplus2(l)
    if mask is not None:
        sp = jnp.where(mask, sp, 0.0)
    hi = sp.astype(jnp.bfloat16)
    lo = (sp - hi.astype(jnp.float32)).astype(jnp.bfloat16)
    csum = (jnp.dot(hi, suffix, preferred_element_type=jnp.float32)
            + jnp.dot(lo, suffix, preferred_element_type=jnp.float32))
    w = jnp.exp2(l - (csum + rsum))
    if mask is not None:
        w = jnp.where(mask, w, 0.0)
    contrib = jnp.dot(w.astype(jnp.bfloat16), v_tile, preferred_element_type=jnp.float32)
    return contrib, rsum + csum[:, 0:1]


def _attention_kernel(q_ref, k_ref, v_ref, zs_ref, o_ref, suffix_ref):
    seq = q_ref.shape[0]
    t = ATT_TILE
    n_blocks = seq // t

    row = lax.broadcasted_iota(jnp.int32, (t, t), 0)
    col = lax.broadcasted_iota(jnp.int32, (t, t), 1)
    suffix_ref[...] = (row >= col).astype(jnp.bfloat16)
    diag_mask = col < row
    lane = lax.broadcasted_iota(jnp.int32, (t, LANES), 1)
    head0 = lane < SB_HEAD_DIM

    def logits(q_h, kt):
        return lax.dot_general(q_h, kt, (((1,), (1,)), ((), ())),
                               preferred_element_type=jnp.float32)

    def q_block(qi, _):
        r0 = pl.multiple_of(qi * t, t)
        q = q_ref[pl.ds(r0, t), :]
        zero = jnp.zeros_like(q)
        q0 = jnp.where(head0, q, zero)
        q1 = jnp.where(head0, zero, q)
        suffix = suffix_ref[...]

        kt = k_ref[pl.ds(r0, t), :]
        vt = v_ref[pl.ds(r0, t), :]
        rs = jnp.zeros((t, 1), jnp.float32)
        acc0, rs0 = _sb_tile(logits(q0, kt), vt, suffix, rs, diag_mask)
        acc1, rs1 = _sb_tile(logits(q1, kt), vt, suffix, rs, diag_mask)

        def key_tile(s, carry):
            acc0, rs0, acc1, rs1 = carry
            c0 = pl.multiple_of((qi - 1 - s) * t, t)
            kt = k_ref[pl.ds(c0, t), :]
            vt = v_ref[pl.ds(c0, t), :]
            d0, rs0 = _sb_tile(logits(q0, kt), vt, suffix, rs0, None)
            d1, rs1 = _sb_tile(logits(q1, kt), vt, suffix, rs1, None)
            return acc0 + d0, rs0, acc1 + d1, rs1

        acc0, _, acc1, _ = lax.fori_loop(0, qi, key_tile, (acc0, rs0, acc1, rs1))
        o = jnp.where(head0, acc0, acc1)
        o_ref[pl.ds(r0, t), :] = (o * zs_ref[pl.ds(r0, t), :].astype(jnp.float32)).astype(o_ref.dtype)
        return 0

    lax.fori_loop(0, n_blocks, q_block, 0)


def _attention(q, k, v, zs):
    bsz, seq, _ = q.shape
    spec = pl.BlockSpec((None, seq, LANES), lambda b, p: (b, 0, p))
    return pl.pallas_call(
        _attention_kernel,
        grid=(bsz, D_MODEL // LANES),
        in_specs=[spec] * 4,
        out_specs=spec,
        out_shape=jax.ShapeDtypeStruct(q.shape, jnp.bfloat16),
        scratch_shapes=[pltpu.VMEM((ATT_TILE, ATT_TILE), jnp.bfloat16)],
        compiler_params=pltpu.CompilerParams(
            dimension_semantics=("arbitrary", "arbitrary"),
            vmem_limit_bytes=V7X_VMEM_LIMIT_BYTES),
        name="sb_attention",
    )(q, k, v, zs)


def _out_kernel(x_ref, a_ref, u_ref, halo_ref, zc_ref, gs_ref, gc_ref,
                ln_in_g_ref, ln_in_b_ref, cw_ref, cb_ref, cg_ref, cbeta_ref,
                w_sb_ref, w_cv_ref, w_out_ref, ln_g_ref, ln_b_ref,
                o_ref, ext_ref, conv_ref, wb_ref, *, tiles_per_seq):
    i = pl.program_id(0)
    tm = x_ref.shape[0]
    n_slabs = D_MODEL // LANES

    @pl.when(i == 0)
    def _():
        for tap in range(CV_KERNEL):
            for s in range(n_slabs):
                wb_ref[s, tap * SUBLANES:(tap + 1) * SUBLANES, :] = jnp.broadcast_to(
                    cw_ref[tap:tap + 1, s * LANES:(s + 1) * LANES], (SUBLANES, LANES))

    first = (i % tiles_per_seq) == 0
    for s in range(n_slabs):
        halo = halo_ref[:, s * LANES:(s + 1) * LANES]
        ext_ref[s, 0:CONV_HALO, :] = jnp.where(first, jnp.zeros_like(halo), halo)
        ext_ref[s, CONV_HALO:, :] = u_ref[:, s * LANES:(s + 1) * LANES]

    base = CONV_HALO - (CV_KERNEL - 1)

    def conv_chunk(c, _):
        r0 = pl.multiple_of(c * CONV_CHUNK, CONV_CHUNK)
        for s in range(n_slabs):
            win = [ext_ref[s, pl.ds(r0 + base + m, SUBLANES, stride=CONV_STRIDE), :]
                   for m in range(CONV_STRIDE + CV_KERNEL - 1)]
            acc = [jnp.zeros((SUBLANES, LANES), jnp.float32) for _ in range(CONV_STRIDE)]
            for tap in range(CV_KERNEL):
                w = wb_ref[s, tap * SUBLANES:(tap + 1) * SUBLANES, :]
                for ph in range(CONV_STRIDE):
                    acc[ph] = acc[ph] + w * win[ph + tap]
            for ph in range(CONV_STRIDE):
                conv_ref[s, pl.ds(r0 + ph, SUBLANES, stride=CONV_STRIDE), :] = acc[ph]
        return 0

    lax.fori_loop(0, tm // CONV_CHUNK, conv_chunk, 0)

    conv = jnp.concatenate([conv_ref[s] for s in range(n_slabs)], axis=-1) + cb_ref[...]
    cn = _layer_norm_f32(conv, cg_ref[...], cbeta_ref[...])
    ucv = cn * _sigmoid(cn) * zc_ref[...].astype(jnp.float32)
    y_cv = jnp.dot(ucv.astype(jnp.bfloat16), w_cv_ref[...], preferred_element_type=jnp.float32)
    y_sb = jnp.dot(a_ref[...], w_sb_ref[...], preferred_element_type=jnp.float32)
    merged = (gs_ref[...].astype(jnp.float32) * y_sb + gc_ref[...].astype(jnp.float32) * y_cv)
    proj = jnp.dot(merged.astype(jnp.bfloat16), w_out_ref[...], preferred_element_type=jnp.float32)
    h0 = _layer_norm_f32(x_ref[...], ln_in_g_ref[...], ln_in_b_ref[...])
    o_ref[...] = _layer_norm_f32(DEEPNORM_ALPHA * h0 + proj, ln_g_ref[...], ln_b_ref[...])


def _out_stage(x2d, a, u, zc, gs, gc, ln_in_g, ln_in_b, conv_w, conv_b, conv_ln_g, conv_ln_b,
               w_sb, w_cv, w_out, ln_g, ln_b, seq):
    n_rows = x2d.shape[0]
    tm = OUT_ROWS
    halo_per_tile = tm // CONV_HALO
    row_spec = pl.BlockSpec((tm, D_MODEL), lambda i: (i, 0))
    halo_spec = pl.BlockSpec((CONV_HALO, D_MODEL),
                             lambda i: (jnp.maximum(i * halo_per_tile - 1, 0), 0))
    vec_spec = pl.BlockSpec((1, D_MODEL), lambda i: (0, 0))
    mat_spec = pl.BlockSpec((D_MODEL, D_MODEL), lambda i: (0, 0))
    return pl.pallas_call(
        functools.partial(_out_kernel, tiles_per_seq=seq // tm),
        grid=(n_rows // tm,),
        in_specs=[row_spec, row_spec, row_spec, halo_spec, row_spec, row_spec, row_spec,
                  vec_spec, vec_spec,
                  pl.BlockSpec((CV_KERNEL, D_MODEL), lambda i: (0, 0)),
                  vec_spec, vec_spec, vec_spec,
                  mat_spec, mat_spec, mat_spec, vec_spec, vec_spec],
        out_specs=row_spec,
        out_shape=jax.ShapeDtypeStruct((n_rows, D_MODEL), jnp.float32),
        scratch_shapes=[pltpu.VMEM((D_MODEL // LANES, tm + CONV_HALO, LANES), jnp.float32),
                        pltpu.VMEM((D_MODEL // LANES, tm, LANES), jnp.float32),
                        pltpu.VMEM((D_MODEL // LANES, CV_KERNEL * SUBLANES, LANES), jnp.float32)],
        compiler_params=pltpu.CompilerParams(
            dimension_semantics=("arbitrary",),
            vmem_limit_bytes=V7X_VMEM_LIMIT_BYTES),
        name="out_stage",
    )(x2d, a, u, u, zc, gs, gc, ln_in_g, ln_in_b, conv_w, conv_b, conv_ln_g, conv_ln_b,
      w_sb, w_cv, w_out, ln_g, ln_b)


def kernel(x, ln_in_g, ln_in_b, w_in, w_sb_proj, conv_w, conv_b, conv_ln_g, conv_ln_b,
           w_cv_proj, w_out, ln_post_g, ln_post_b):
    bsz, seq, d = x.shape
    assert d == D_MODEL and w_in.shape == (DEPTH, D_MODEL, N_SPLITS * D_MODEL)
    assert seq % ATT_TILE == 0 and seq % OUT_ROWS == 0 and (bsz * seq) % PROJ_ROWS == 0
    x2d = x.reshape(bsz * seq, d)
    row = lambda p: p.reshape(1, D_MODEL)
    bf16 = lambda w: w.astype(jnp.bfloat16)

    q, k, v, zs, u, zc, gs, gc = _in_proj(x2d, row(ln_in_g), row(ln_in_b), bf16(w_in[0]))
    to_seq = lambda t: t.reshape(bsz, seq, D_MODEL)
    a = _attention(to_seq(q), to_seq(k), to_seq(v), to_seq(zs)).reshape(bsz * seq, D_MODEL)
    out = _out_stage(x2d, a, u, zc, gs, gc, row(ln_in_g), row(ln_in_b), conv_w[0], row(conv_b[0]),
                     row(conv_ln_g[0]), row(conv_ln_b[0]), bf16(w_sb_proj[0]), bf16(w_cv_proj[0]),
                     bf16(w_out[0]), row(ln_post_g[0]), row(ln_post_b[0]), seq)
    return out.reshape(bsz, seq, d)
```

```python
import functools
import math

import jax
import jax.numpy as jnp
from jax import lax
from jax.experimental import pallas as pl
from jax.experimental.pallas import tpu as pltpu

D_MODEL = 1024
SB_HEADS = 16
SB_HEAD_DIM = 64
CV_KERNEL = 31
LN_EPS = 1e-5
DEPTH = 1
DEEPNORM_ALPHA = (2.0 * DEPTH) ** 0.25
N_SPLITS = 9
GLU_COL_GROUP = 4
ACT_GROUPS = 7
ACT_Q, ACT_K, ACT_V, ACT_ZS, ACT_ZC, ACT_GS, ACT_GC = range(ACT_GROUPS)

LOG2E = 1.4426950408889634
Q_SCALE = LOG2E / math.sqrt(SB_HEAD_DIM)

V7X_VMEM_LIMIT_BYTES = 56 * 1024 * 1024
LANES = 128
SUBLANES = 8

PROJ_ROWS = 512
ATT_TILE = 256
ATT_HEADS = 4
OUT_ROWS = 256
CONV_HALO = 32
CONV_STRIDE = 4
CONV_CHUNK = CONV_STRIDE * SUBLANES


def _layer_norm_f32(x, g, b):
    mu = jnp.mean(x, axis=-1, keepdims=True)
    xc = x - mu
    var = jnp.mean(xc * xc, axis=-1, keepdims=True)
    return xc * lax.rsqrt(var + LN_EPS) * g + b


def _sigmoid(x):
    return 1.0 / (1.0 + jnp.exp2(x * (-LOG2E)))


def _proj_kernel(x_ref, g_ref, b_ref, w_ref, act_ref, hn_ref, raw_ref):
    s = pl.program_id(0)
    j = jnp.minimum(s, pl.num_programs(0) - 2) % ACT_GROUPS
    jp = jnp.maximum(s - 1, 0) % ACT_GROUPS

    @pl.when(s == 0)
    def _():
        raw_ref[...] = jnp.zeros_like(raw_ref)

    @pl.when(j == 0)
    def _():
        hn_ref[...] = _layer_norm_f32(x_ref[...], g_ref[...], b_ref[...]).astype(hn_ref.dtype)

    raw = raw_ref[...]
    sig = _sigmoid(raw)
    alpha = jnp.where(jp == ACT_Q, Q_SCALE, jnp.where(jp <= ACT_V, 1.0, 0.0)).astype(jnp.float32)
    beta = jnp.where((jp == ACT_ZS) | (jp == ACT_ZC), 1.0, 0.0).astype(jnp.float32)
    gamma = jnp.where(jp >= ACT_GS, 1.0, 0.0).astype(jnp.float32)
    act_ref[...] = (raw * (alpha + beta * sig) + gamma * sig).astype(act_ref.dtype)
    raw_ref[...] = jnp.dot(hn_ref[...], w_ref[...], preferred_element_type=jnp.float32)


def _proj(x2d, ln_g, ln_b, w_in_bf16):
    n_rows = x2d.shape[0]
    tm = PROJ_ROWS
    n_work = (n_rows // tm) * ACT_GROUPS
    cur = lambda s: jnp.minimum(s, n_work - 1)
    prev = lambda s: jnp.maximum(s - 1, 0)
    row_spec = pl.BlockSpec((tm, D_MODEL), lambda s: (cur(s) // ACT_GROUPS, 0))
    vec_spec = pl.BlockSpec((1, D_MODEL), lambda s: (0, 0))

    def w_index(s):
        j = cur(s) % ACT_GROUPS
        return 0, j + 2 * (j // GLU_COL_GROUP)

    return pl.pallas_call(
        _proj_kernel,
        grid=(n_work + 1,),
        in_specs=[row_spec, vec_spec, vec_spec, pl.BlockSpec((D_MODEL, D_MODEL), w_index)],
        out_specs=[pl.BlockSpec((tm, D_MODEL),
                                lambda s: (prev(s) // ACT_GROUPS, prev(s) % ACT_GROUPS)),
                   row_spec],
        out_shape=[jax.ShapeDtypeStruct((n_rows, ACT_GROUPS * D_MODEL), jnp.bfloat16),
                   jax.ShapeDtypeStruct((n_rows, D_MODEL), jnp.bfloat16)],
        scratch_shapes=[pltpu.VMEM((tm, D_MODEL), jnp.float32)],
        compiler_params=pltpu.CompilerParams(
            dimension_semantics=("arbitrary",),
            vmem_limit_bytes=V7X_VMEM_LIMIT_BYTES),
        name="in_proj",
    )(x2d, ln_g, ln_b, w_in_bf16)


def _glu_kernel(hn_ref, w_ref, u_ref):
    acc = jnp.dot(hn_ref[...], w_ref[...], preferred_element_type=jnp.float32)
    u_ref[...] = acc[:, :D_MODEL] * _sigmoid(acc[:, D_MODEL:])


def _glu_proj(hn, w_in_bf16):
    n_rows = hn.shape[0]
    tm = PROJ_ROWS
    return pl.pallas_call(
        _glu_kernel,
        grid=(n_rows // tm,),
        in_specs=[pl.BlockSpec((tm, D_MODEL), lambda i: (i, 0)),
                  pl.BlockSpec((D_MODEL, 2 * D_MODEL), lambda i: (0, GLU_COL_GROUP // 2))],
        out_specs=pl.BlockSpec((tm, D_MODEL), lambda i: (i, 0)),
        out_shape=jax.ShapeDtypeStruct((n_rows, D_MODEL), jnp.float32),
        compiler_params=pltpu.CompilerParams(
            dimension_semantics=("arbitrary",),
            vmem_limit_bytes=V7X_VMEM_LIMIT_BYTES),
        name="glu_proj",
    )(hn, w_in_bf16)


def _softplus2_and_logsig2(l):
    hi = jnp.maximum(l, 0.0)
    lo = jnp.minimum(l, 0.0)
    g = jnp.log2(1.0 + jnp.exp2(lo - hi))
    return hi + g, lo - g


def _attention_kernel(q_ref, k_ref, v_ref, zs_ref, o_ref, suffix_ref, vm_ref, acc_ref, rb_ref):
    seq = q_ref.shape[0]
    t = ATT_TILE
    nh = ATT_HEADS
    n_blocks = seq // t

    row = lax.broadcasted_iota(jnp.int32, (t, t), 0)
    col = lax.broadcasted_iota(jnp.int32, (t, t), 1)
    suffix_ref[...] = (row > col).astype(jnp.bfloat16)
    diag_mask = (col < row)[None]
    lane_head = lax.broadcasted_iota(jnp.int32, (t, nh * SB_HEAD_DIM), 1) // SB_HEAD_DIM

    def per_head(x):
        zero = jnp.zeros_like(x)
        return jnp.concatenate([jnp.where(lane_head == h, x, zero) for h in range(nh)], axis=0)

    for j in range(n_blocks):
        vm_ref[j] = per_head(v_ref[j * t:(j + 1) * t, :])

    def tile(q_stack, j, rb, diag):
        kt = k_ref[pl.ds(pl.multiple_of(j * t, t), t), :]
        l = lax.dot_general(q_stack, kt, (((1,), (1,)), ((), ())),
                            preferred_element_type=jnp.float32)
        sp, own = _softplus2_and_logsig2(l)
        if diag:
            sp = jnp.where(diag_mask, sp.reshape(nh, t, t), 0.0).reshape(nh * t, t)
        later = jnp.dot(sp.astype(jnp.bfloat16), suffix_ref[...], preferred_element_type=jnp.float32)
        if diag:
            tot = [later[:, :LANES], later[:, LANES:]]
        else:
            tot = [later[:, :LANES] + rb, later[:, LANES:] + rb]
        rb = jnp.broadcast_to((tot[0] + sp[:, :LANES])[:, 0:1], (nh * t, LANES))
        w = jnp.exp2(own - jnp.concatenate(tot, axis=1))
        if diag:
            w = jnp.where(diag_mask, w.reshape(nh, t, t), 0.0).reshape(nh * t, t)
        wb = w.astype(jnp.bfloat16)
        w_cat = jnp.concatenate([wb[h * t:(h + 1) * t] for h in range(nh)], axis=1)
        return jnp.dot(w_cat, vm_ref[j], preferred_element_type=jnp.float32), rb

    def q_block(qi, _):
        r0 = pl.multiple_of(qi * t, t)
        q_stack = per_head(q_ref[pl.ds(r0, t), :])
        acc_ref[...], rb_ref[...] = tile(q_stack, qi, None, True)

        @pl.when(qi % 2 == 1)
        def _():
            contrib, rb_ref[...] = tile(q_stack, qi - 1, rb_ref[...], False)
            acc_ref[...] += contrib

        def key_tile_pair(s, _):
            j = 2 * (qi // 2 - s) - 1
            c1, rb = tile(q_stack, j, rb_ref[...], False)
            c2, rb_ref[...] = tile(q_stack, j - 1, rb, False)
            acc_ref[...] += c1 + c2
            return 0

        lax.fori_loop(0, qi // 2, key_tile_pair, 0)
        gate = zs_ref[pl.ds(r0, t), :].astype(jnp.float32)
        o_ref[pl.ds(r0, t), :] = (acc_ref[...] * gate).astype(o_ref.dtype)
        return 0

    lax.fori_loop(0, n_blocks, q_block, 0)


def _attention(act3):
    bsz, seq, _ = act3.shape
    lanes = ATT_HEADS * SB_HEAD_DIM
    per_group = D_MODEL // lanes
    group_spec = lambda g: pl.BlockSpec((None, seq, lanes), lambda b, p: (b, 0, g * per_group + p))
    return pl.pallas_call(
        _attention_kernel,
        grid=(bsz, per_group),
        in_specs=[group_spec(ACT_Q), group_spec(ACT_K), group_spec(ACT_V), group_spec(ACT_ZS)],
        out_specs=pl.BlockSpec((None, seq, lanes), lambda b, p: (b, 0, p)),
        out_shape=jax.ShapeDtypeStruct((bsz, seq, D_MODEL), jnp.bfloat16),
        scratch_shapes=[pltpu.VMEM((ATT_TILE, ATT_TILE), jnp.bfloat16),
                        pltpu.VMEM((seq // ATT_TILE, ATT_HEADS * ATT_TILE, lanes), jnp.bfloat16),
                        pltpu.VMEM((ATT_TILE, lanes), jnp.float32),
                        pltpu.VMEM((ATT_HEADS * ATT_TILE, LANES), jnp.float32)],
        compiler_params=pltpu.CompilerParams(
            dimension_semantics=("arbitrary", "arbitrary"),
            vmem_limit_bytes=V7X_VMEM_LIMIT_BYTES),
        name="sb_attention",
    )(act3, act3, act3, act3)


def _out_kernel(x_ref, a_ref, u_ref, halo_ref, zc_ref, gs_ref, gc_ref,
                ln_in_g_ref, ln_in_b_ref, cw_ref, cb_ref, cg_ref, cbeta_ref,
                w_sb_ref, w_cv_ref, w_out_ref, ln_g_ref, ln_b_ref,
                o_ref, ext_ref, conv_ref, wb_ref, *, tiles_per_seq):
    i = pl.program_id(0)
    tm = x_ref.shape[0]
    n_slabs = D_MODEL // LANES

    @pl.when(i == 0)
    def _():
        for tap in range(CV_KERNEL):
            for s in range(n_slabs):
                wb_ref[s, tap * SUBLANES:(tap + 1) * SUBLANES, :] = jnp.broadcast_to(
                    cw_ref[tap:tap + 1, s * LANES:(s + 1) * LANES], (SUBLANES, LANES))

    first = (i % tiles_per_seq) == 0
    for s in range(n_slabs):
        halo = halo_ref[:, s * LANES:(s + 1) * LANES]
        ext_ref[s, 0:CONV_HALO, :] = jnp.where(first, jnp.zeros_like(halo), halo)
        ext_ref[s, CONV_HALO:, :] = u_ref[:, s * LANES:(s + 1) * LANES]

    base = CONV_HALO - (CV_KERNEL - 1)

    def conv_chunk(c, _):
        r0 = pl.multiple_of(c * CONV_CHUNK, CONV_CHUNK)
        for s in range(n_slabs):
            win = [ext_ref[s, pl.ds(r0 + base + m, SUBLANES, stride=CONV_STRIDE), :]
                   for m in range(CONV_STRIDE + CV_KERNEL - 1)]
            acc = [jnp.zeros((SUBLANES, LANES), jnp.float32) for _ in range(CONV_STRIDE)]
            for tap in range(CV_KERNEL):
                w = wb_ref[s, tap * SUBLANES:(tap + 1) * SUBLANES, :]
                for ph in range(CONV_STRIDE):
                    acc[ph] = acc[ph] + w * win[ph + tap]
            for ph in range(CONV_STRIDE):
                conv_ref[s, pl.ds(r0 + ph, SUBLANES, stride=CONV_STRIDE), :] = acc[ph]
        return 0

    lax.fori_loop(0, tm // CONV_CHUNK, conv_chunk, 0)

    conv = jnp.concatenate([conv_ref[s] for s in range(n_slabs)], axis=-1) + cb_ref[...]
    cn = _layer_norm_f32(conv, cg_ref[...], cbeta_ref[...])
    ucv = cn * _sigmoid(cn) * zc_ref[...].astype(jnp.float32)
    y_cv = jnp.dot(ucv.astype(jnp.bfloat16), w_cv_ref[...], preferred_element_type=jnp.float32)
    y_sb = jnp.dot(a_ref[...], w_sb_ref[...], preferred_element_type=jnp.float32)
    merged = (gs_ref[...].astype(jnp.float32) * y_sb + gc_ref[...].astype(jnp.float32) * y_cv)
    proj = jnp.dot(merged.astype(jnp.bfloat16), w_out_ref[...], preferred_element_type=jnp.float32)
    h0 = _layer_norm_f32(x_ref[...], ln_in_g_ref[...], ln_in_b_ref[...])
    o_ref[...] = _layer_norm_f32(DEEPNORM_ALPHA * h0 + proj, ln_g_ref[...], ln_b_ref[...])


def _out_stage(x2d, a, u, act, ln_in_g, ln_in_b, conv_w, conv_b, conv_ln_g, conv_ln_b,
               w_sb, w_cv, w_out, ln_g, ln_b, seq):
    n_rows = x2d.shape[0]
    tm = OUT_ROWS
    halo_per_tile = tm // CONV_HALO
    row_spec = pl.BlockSpec((tm, D_MODEL), lambda i: (i, 0))
    halo_spec = pl.BlockSpec((CONV_HALO, D_MODEL),
                             lambda i: (jnp.maximum(i * halo_per_tile - 1, 0), 0))
    act_spec = lambda g: pl.BlockSpec((tm, D_MODEL), lambda i: (i, g))
    vec_spec = pl.BlockSpec((1, D_MODEL), lambda i: (0, 0))
    mat_spec = pl.BlockSpec((D_MODEL, D_MODEL), lambda i: (0, 0))
    return pl.pallas_call(
        functools.partial(_out_kernel, tiles_per_seq=seq // tm),
        grid=(n_rows // tm,),
        in_specs=[row_spec, row_spec, row_spec, halo_spec,
                  act_spec(ACT_ZC), act_spec(ACT_GS), act_spec(ACT_GC),
                  vec_spec, vec_spec,
                  pl.BlockSpec((CV_KERNEL, D_MODEL), lambda i: (0, 0)),
                  vec_spec, vec_spec, vec_spec,
                  mat_spec, mat_spec, mat_spec, vec_spec, vec_spec],
        out_specs=row_spec,
        out_shape=jax.ShapeDtypeStruct((n_rows, D_MODEL), jnp.float32),
        scratch_shapes=[pltpu.VMEM((D_MODEL // LANES, tm + CONV_HALO, LANES), jnp.float32),
                        pltpu.VMEM((D_MODEL // LANES, tm, LANES), jnp.float32),
                        pltpu.VMEM((D_MODEL // LANES, CV_KERNEL * SUBLANES, LANES), jnp.float32)],
        compiler_params=pltpu.CompilerParams(
            dimension_semantics=("arbitrary",),
            vmem_limit_bytes=V7X_VMEM_LIMIT_BYTES),
        name="out_stage",
    )(x2d, a, u, u, act, act, act, ln_in_g, ln_in_b, conv_w, conv_b, conv_ln_g, conv_ln_b,
      w_sb, w_cv, w_out, ln_g, ln_b)


def kernel(x, ln_in_g, ln_in_b, w_in, w_sb_proj, conv_w, conv_b, conv_ln_g, conv_ln_b,
           w_cv_proj, w_out, ln_post_g, ln_post_b):
    bsz, seq, d = x.shape
    assert d == D_MODEL and w_in.shape == (DEPTH, D_MODEL, N_SPLITS * D_MODEL)
    assert seq % ATT_TILE == 0 and seq % OUT_ROWS == 0 and (bsz * seq) % PROJ_ROWS == 0
    x2d = x.reshape(bsz * seq, d)
    row = lambda p: p.reshape(1, D_MODEL)
    bf16 = lambda w: w.astype(jnp.bfloat16)

    w_in_bf16 = bf16(w_in[0])
    act, hn = _proj(x2d, row(ln_in_g), row(ln_in_b), w_in_bf16)
    u = _glu_proj(hn, w_in_bf16)
    a = _attention(act.reshape(bsz, seq, ACT_GROUPS * D_MODEL)).reshape(bsz * seq, D_MODEL)
    out = _out_stage(x2d, a, u, act, row(ln_in_g), row(ln_in_b), conv_w[0], row(conv_b[0]),
                     row(conv_ln_g[0]), row(conv_ln_b[0]), bf16(w_sb_proj[0]), bf16(w_cv_proj[0]),
                     bf16(w_out[0]), row(ln_post_g[0]), row(ln_post_b[0]), seq)
    return out.reshape(bsz, seq, d)
```

```python
import functools
import math

import jax
import jax.numpy as jnp
from jax import lax
from jax.experimental import pallas as pl
from jax.experimental.pallas import tpu as pltpu

D_MODEL = 1024
SB_HEADS = 16
SB_HEAD_DIM = 64
CV_KERNEL = 31
LN_EPS = 1e-5
DEPTH = 1
DEEPNORM_ALPHA = (2.0 * DEPTH) ** 0.25
N_SPLITS = 9
GLU_COL_GROUP = 4
ACT_GROUPS = 7
ACT_Q, ACT_K, ACT_V, ACT_ZS, ACT_ZC, ACT_GS, ACT_GC = range(ACT_GROUPS)

LOG2E = 1.4426950408889634
Q_SCALE = LOG2E / math.sqrt(SB_HEAD_DIM)

V7X_VMEM_LIMIT_BYTES = 56 * 1024 * 1024
LANES = 128
SUBLANES = 8

PROJ_ROWS = 1024
ATT_TILE = 256
ATT_HEADS = 4
MASKED_EXPONENT = -1e30
OUT_ROWS = 256
CONV_HALO = 32
CONV_STRIDE = 4
CONV_CHUNK = CONV_STRIDE * SUBLANES


def _layer_norm_f32(x, g, b):
    mu = jnp.mean(x, axis=-1, keepdims=True)
    xc = x - mu
    var = jnp.mean(xc * xc, axis=-1, keepdims=True)
    return xc * lax.rsqrt(var + LN_EPS) * g + b


def _sigmoid(x):
    return 1.0 / (1.0 + jnp.exp2(x * (-LOG2E)))


def _proj_kernel(x_ref, g_ref, b_ref, w_ref, act_ref, hn_ref, raw_ref):
    s = pl.program_id(0)
    j = jnp.minimum(s, pl.num_programs(0) - 2) % ACT_GROUPS
    jp = jnp.maximum(s - 1, 0) % ACT_GROUPS

    @pl.when(s == 0)
    def _():
        raw_ref[...] = jnp.zeros_like(raw_ref)

    @pl.when(j == 0)
    def _():
        hn_ref[...] = _layer_norm_f32(x_ref[...], g_ref[...], b_ref[...]).astype(hn_ref.dtype)

    raw = raw_ref[...]
    sig = _sigmoid(raw)
    alpha = jnp.where(jp == ACT_Q, Q_SCALE, jnp.where(jp <= ACT_V, 1.0, 0.0)).astype(jnp.float32)
    beta = jnp.where((jp == ACT_ZS) | (jp == ACT_ZC), 1.0, 0.0).astype(jnp.float32)
    gamma = jnp.where(jp >= ACT_GS, 1.0, 0.0).astype(jnp.float32)
    act_ref[...] = (raw * (alpha + beta * sig) + gamma * sig).astype(act_ref.dtype)
    raw_ref[...] = jnp.dot(hn_ref[...], w_ref[...], preferred_element_type=jnp.float32)


def _proj(x2d, ln_g, ln_b, w_in_bf16):
    n_rows = x2d.shape[0]
    tm = PROJ_ROWS
    n_work = (n_rows // tm) * ACT_GROUPS
    cur = lambda s: jnp.minimum(s, n_work - 1)
    prev = lambda s: jnp.maximum(s - 1, 0)
    row_spec = pl.BlockSpec((tm, D_MODEL), lambda s: (cur(s) // ACT_GROUPS, 0))
    vec_spec = pl.BlockSpec((1, D_MODEL), lambda s: (0, 0))

    def w_index(s):
        j = cur(s) % ACT_GROUPS
        return 0, j + 2 * (j // GLU_COL_GROUP)

    return pl.pallas_call(
        _proj_kernel,
        grid=(n_work + 1,),
        in_specs=[row_spec, vec_spec, vec_spec, pl.BlockSpec((D_MODEL, D_MODEL), w_index)],
        out_specs=[pl.BlockSpec((tm, D_MODEL),
                                lambda s: (prev(s) // ACT_GROUPS, prev(s) % ACT_GROUPS)),
                   row_spec],
        out_shape=[jax.ShapeDtypeStruct((n_rows, ACT_GROUPS * D_MODEL), jnp.bfloat16),
                   jax.ShapeDtypeStruct((n_rows, D_MODEL), jnp.bfloat16)],
        scratch_shapes=[pltpu.VMEM((tm, D_MODEL), jnp.float32)],
        compiler_params=pltpu.CompilerParams(
            dimension_semantics=("arbitrary",),
            vmem_limit_bytes=V7X_VMEM_LIMIT_BYTES),
        name="in_proj",
    )(x2d, ln_g, ln_b, w_in_bf16)


def _glu_kernel(hn_ref, w_ref, u_ref):
    acc = jnp.dot(hn_ref[...], w_ref[...], preferred_element_type=jnp.float32)
    u_ref[...] = acc[:, :D_MODEL] * _sigmoid(acc[:, D_MODEL:])


def _glu_proj(hn, w_in_bf16):
    n_rows = hn.shape[0]
    tm = PROJ_ROWS
    return pl.pallas_call(
        _glu_kernel,
        grid=(n_rows // tm,),
        in_specs=[pl.BlockSpec((tm, D_MODEL), lambda i: (i, 0)),
                  pl.BlockSpec((D_MODEL, 2 * D_MODEL), lambda i: (0, GLU_COL_GROUP // 2))],
        out_specs=pl.BlockSpec((tm, D_MODEL), lambda i: (i, 0)),
        out_shape=jax.ShapeDtypeStruct((n_rows, D_MODEL), jnp.float32),
        compiler_params=pltpu.CompilerParams(
            dimension_semantics=("arbitrary",),
            vmem_limit_bytes=V7X_VMEM_LIMIT_BYTES),
        name="glu_proj",
    )(hn, w_in_bf16)


def _softplus2_and_logsig2(l):
    g = jnp.log2(1.0 + jnp.exp2(jnp.minimum(l, -l)))
    sp = jnp.maximum(l, 0.0) + g
    return sp, l - sp


def _attention_kernel(q_ref, k_ref, v_ref, zs_ref, o_ref,
                      suffix_ref, qt_ref, vt_ref, acc_ref, rs_ref, z_ref, sp_ref, sp0_ref, own_ref):
    seq = q_ref.shape[0]
    t = ATT_TILE
    nh = ATT_HEADS
    n_blocks = seq // t
    n_diag_pairs = n_blocks // 2
    n_pairs = n_diag_pairs + n_blocks * (n_blocks - 1) // 4

    key = lax.broadcasted_iota(jnp.int32, (t, t), 0)
    qry = lax.broadcasted_iota(jnp.int32, (t, t), 1)
    suffix_ref[...] = (qry > key).astype(jnp.bfloat16)
    diag_mask = jnp.concatenate([key < qry] * nh, axis=1)
    dim_head = lax.broadcasted_iota(jnp.int32, (nh * SB_HEAD_DIM, t), 0) // SB_HEAD_DIM

    def per_head_t(x):
        xt = x.astype(jnp.float32).T.astype(x.dtype)
        zero = jnp.zeros_like(xt)
        return jnp.concatenate([jnp.where(dim_head == h, xt, zero) for h in range(nh)], axis=1)

    for j in range(n_blocks):
        qt_ref[j] = per_head_t(q_ref[j * t:(j + 1) * t, :])
        vt_ref[j] = per_head_t(v_ref[j * t:(j + 1) * t, :])
    acc_ref[...] = jnp.zeros_like(acc_ref)
    z_ref[0] = jnp.full_like(z_ref[0], MASKED_EXPONENT)

    def tile_of(p, c):
        n = 2 * (p - n_diag_pairs) + c
        qi = 1 + sum(jnp.where(n >= k * (k + 1) // 2, 1, 0) for k in range(1, n_blocks - 1))
        j = qi - 1 - (n - qi * (qi - 1) // 2)
        on_diag = p < n_diag_pairs
        return jnp.where(on_diag, 2 * p + c, qi), jnp.where(on_diag, 2 * p + c, j)

    def logits(p):
        ls = []
        for c in range(2):
            qi, j = tile_of(p, c)
            kt = k_ref[pl.ds(pl.multiple_of(j * t, t), t), :]
            ls.append(jnp.dot(kt, qt_ref[qi], preferred_element_type=jnp.float32))
        return ls

    def exponent_stage(p, ls, diag):
        for c in range(2):
            qi, j = tile_of(p, c)
            sp, own = _softplus2_and_logsig2(ls[c])
            if diag:
                sp = jnp.where(diag_mask, sp, 0.0)
            tot = jnp.dot(suffix_ref[...], sp.astype(jnp.bfloat16), preferred_element_type=jnp.float32)
            if not diag:
                tot = (tot.reshape(t // SUBLANES, SUBLANES, nh * t) + rs_ref[qi][None]).reshape(t, nh * t)
            rs_ref[qi] = jnp.broadcast_to(tot[0:1] + sp[0:1], (SUBLANES, nh * t))
            z = own - tot
            if diag:
                z = jnp.where(diag_mask, z, MASKED_EXPONENT)
            z_ref[0, c] = z

    def weight_stage(p, slot):
        for c in range(2):
            qi, j = tile_of(p, c)
            wb = jnp.exp2(z_ref[slot, c]).astype(jnp.bfloat16)
            w_stack = jnp.concatenate([wb[:, h * t:(h + 1) * t] for h in range(nh)], axis=0)
            acc_ref[qi] += jnp.dot(vt_ref[j], w_stack, preferred_element_type=jnp.float32)

    def softplus_stage(p, slot):
        for c, l in enumerate(logits(p)):
            sp, own = _softplus2_and_logsig2(l)
            sp_ref[slot, c] = sp.astype(jnp.bfloat16)
            sp0_ref[slot, c] = sp[0:SUBLANES]
            own_ref[slot, c] = own

    def later_sum_stage(p, slot):
        for c in range(2):
            qi, _ = tile_of(p, c)
            tot = jnp.dot(suffix_ref[...], sp_ref[slot, c], preferred_element_type=jnp.float32)
            tot = (tot.reshape(t // SUBLANES, SUBLANES, nh * t) + rs_ref[qi][None]).reshape(t, nh * t)
            rs_ref[qi] = jnp.broadcast_to(tot[0:1] + sp0_ref[slot, c][0:1], (SUBLANES, nh * t))
            z_ref[slot, c] = own_ref[slot, c] - tot

    def diag_pair(p, _):
        ls = logits(p)
        weight_stage(jnp.maximum(p - 1, 0), 0)
        exponent_stage(p, ls, True)
        return 0

    first = n_diag_pairs

    def two_full_pairs(i, _):
        p = first + 2 + 2 * i
        weight_stage(p - 2, 0)
        softplus_stage(p, 0)
        later_sum_stage(p - 1, 1)
        weight_stage(p - 1, 1)
        softplus_stage(p + 1, 1)
        later_sum_stage(p, 0)
        return 0

    lax.fori_loop(0, first, diag_pair, 0)
    weight_stage(first - 1, 0)
    softplus_stage(first, 0)
    softplus_stage(first + 1, 1)
    later_sum_stage(first, 0)
    lax.fori_loop(0, (n_pairs - first - 2) // 2, two_full_pairs, 0)
    later_sum_stage(n_pairs - 1, 1)
    weight_stage(n_pairs - 2, 0)
    weight_stage(n_pairs - 1, 1)

    for qi in range(n_blocks):
        gate = zs_ref[qi * t:(qi + 1) * t, :].astype(jnp.float32)
        o_ref[qi * t:(qi + 1) * t, :] = (acc_ref[qi].T * gate).astype(o_ref.dtype)


def _attention(act3):
    bsz, seq, _ = act3.shape
    lanes = ATT_HEADS * SB_HEAD_DIM
    per_group = D_MODEL // lanes
    n_blocks = seq // ATT_TILE
    assert n_blocks % 4 == 0
    group_spec = lambda g: pl.BlockSpec((None, seq, lanes), lambda b, p: (b, 0, g * per_group + p))
    return pl.pallas_call(
        _attention_kernel,
        grid=(bsz, per_group),
        in_specs=[group_spec(ACT_Q), group_spec(ACT_K), group_spec(ACT_V), group_spec(ACT_ZS)],
        out_specs=pl.BlockSpec((None, seq, lanes), lambda b, p: (b, 0, p)),
        out_shape=jax.ShapeDtypeStruct((bsz, seq, D_MODEL), jnp.bfloat16),
        scratch_shapes=[pltpu.VMEM((ATT_TILE, ATT_TILE), jnp.bfloat16),
                        pltpu.VMEM((n_blocks, lanes, ATT_HEADS * ATT_TILE), jnp.bfloat16),
                        pltpu.VMEM((n_blocks, lanes, ATT_HEADS * ATT_TILE), jnp.bfloat16),
                        pltpu.VMEM((n_blocks, lanes, ATT_TILE), jnp.float32),
                        pltpu.VMEM((n_blocks, SUBLANES, ATT_HEADS * ATT_TILE), jnp.float32),
                        pltpu.VMEM((2, 2, ATT_TILE, ATT_HEADS * ATT_TILE), jnp.float32),
                        pltpu.VMEM((2, 2, ATT_TILE, ATT_HEADS * ATT_TILE), jnp.bfloat16),
                        pltpu.VMEM((2, 2, SUBLANES, ATT_HEADS * ATT_TILE), jnp.float32),
                        pltpu.VMEM((2, 2, ATT_TILE, ATT_HEADS * ATT_TILE), jnp.float32)],
        compiler_params=pltpu.CompilerParams(
            dimension_semantics=("arbitrary", "arbitrary"),
            vmem_limit_bytes=V7X_VMEM_LIMIT_BYTES),
        name="sb_attention",
    )(act3, act3, act3, act3)


def _out_kernel(x_ref, a_ref, u_ref, halo_ref, zc_ref, gs_ref, gc_ref,
                ln_in_g_ref, ln_in_b_ref, cw_ref, cb_ref, cg_ref, cbeta_ref,
                w_sb_ref, w_cv_ref, w_out_ref, ln_g_ref, ln_b_ref,
                o_ref, ext_ref, conv_ref, wb_ref, *, tiles_per_seq):
    i = pl.program_id(0)
    tm = x_ref.shape[0]
    n_slabs = D_MODEL // LANES

    @pl.when(i == 0)
    def _():
        for tap in range(CV_KERNEL):
            for s in range(n_slabs):
                wb_ref[s, tap * SUBLANES:(tap + 1) * SUBLANES, :] = jnp.broadcast_to(
                    cw_ref[tap:tap + 1, s * LANES:(s + 1) * LANES], (SUBLANES, LANES))

    first = (i % tiles_per_seq) == 0
    for s in range(n_slabs):
        halo = halo_ref[:, s * LANES:(s + 1) * LANES]
        ext_ref[s, 0:CONV_HALO, :] = jnp.where(first, jnp.zeros_like(halo), halo)
        ext_ref[s, CONV_HALO:, :] = u_ref[:, s * LANES:(s + 1) * LANES]

    base = CONV_HALO - (CV_KERNEL - 1)

    def conv_chunk(c, _):
        r0 = pl.multiple_of(c * CONV_CHUNK, CONV_CHUNK)
        for s in range(n_slabs):
            win = [ext_ref[s, pl.ds(r0 + base + m, SUBLANES, stride=CONV_STRIDE), :]
                   for m in range(CONV_STRIDE + CV_KERNEL - 1)]
            acc = [jnp.zeros((SUBLANES, LANES), jnp.float32) for _ in range(CONV_STRIDE)]
            for tap in range(CV_KERNEL):
                w = wb_ref[s, tap * SUBLANES:(tap + 1) * SUBLANES, :]
                for ph in range(CONV_STRIDE):
                    acc[ph] = acc[ph] + w * win[ph + tap]
            for ph in range(CONV_STRIDE):
                conv_ref[s, pl.ds(r0 + ph, SUBLANES, stride=CONV_STRIDE), :] = acc[ph]
        return 0

    lax.fori_loop(0, tm // CONV_CHUNK, conv_chunk, 0)

    conv = jnp.concatenate([conv_ref[s] for s in range(n_slabs)], axis=-1) + cb_ref[...]
    cn = _layer_norm_f32(conv, cg_ref[...], cbeta_ref[...])
    ucv = cn * _sigmoid(cn) * zc_ref[...].astype(jnp.float32)
    y_cv = jnp.dot(ucv.astype(jnp.bfloat16), w_cv_ref[...], preferred_element_type=jnp.float32)
    y_sb = jnp.dot(a_ref[...], w_sb_ref[...], preferred_element_type=jnp.float32)
    merged = (gs_ref[...].astype(jnp.float32) * y_sb + gc_ref[...].astype(jnp.float32) * y_cv)
    proj = jnp.dot(merged.astype(jnp.bfloat16), w_out_ref[...], preferred_element_type=jnp.float32)
    h0 = _layer_norm_f32(x_ref[...], ln_in_g_ref[...], ln_in_b_ref[...])
    o_ref[...] = _layer_norm_f32(DEEPNORM_ALPHA * h0 + proj, ln_g_ref[...], ln_b_ref[...])


def _out_stage(x2d, a, u, act, ln_in_g, ln_in_b, conv_w, conv_b, conv_ln_g, conv_ln_b,
               w_sb, w_cv, w_out, ln_g, ln_b, seq):
    n_rows = x2d.shape[0]
    tm = OUT_ROWS
    halo_per_tile = tm // CONV_HALO
    row_spec = pl.BlockSpec((tm, D_MODEL), lambda i: (i, 0))
    halo_spec = pl.BlockSpec((CONV_HALO, D_MODEL),
                             lambda i: (jnp.maximum(i * halo_per_tile - 1, 0), 0))
    act_spec = lambda g: pl.BlockSpec((tm, D_MODEL), lambda i: (i, g))
    vec_spec = pl.BlockSpec((1, D_MODEL), lambda i: (0, 0))
    mat_spec = pl.BlockSpec((D_MODEL, D_MODEL), lambda i: (0, 0))
    return pl.pallas_call(
        functools.partial(_out_kernel, tiles_per_seq=seq // tm),
        grid=(n_rows // tm,),
        in_specs=[row_spec, row_spec, row_spec, halo_spec,
                  act_spec(ACT_ZC), act_spec(ACT_GS), act_spec(ACT_GC),
                  vec_spec, vec_spec,
                  pl.BlockSpec((CV_KERNEL, D_MODEL), lambda i: (0, 0)),
                  vec_spec, vec_spec, vec_spec,
                  mat_spec, mat_spec, mat_spec, vec_spec, vec_spec],
        out_specs=row_spec,
        out_shape=jax.ShapeDtypeStruct((n_rows, D_MODEL), jnp.float32),
        scratch_shapes=[pltpu.VMEM((D_MODEL // LANES, tm + CONV_HALO, LANES), jnp.float32),
                        pltpu.VMEM((D_MODEL // LANES, tm, LANES), jnp.float32),
                        pltpu.VMEM((D_MODEL // LANES, CV_KERNEL * SUBLANES, LANES), jnp.float32)],
        compiler_params=pltpu.CompilerParams(
            dimension_semantics=("arbitrary",),
            vmem_limit_bytes=V7X_VMEM_LIMIT_BYTES),
        name="out_stage",
    )(x2d, a, u, u, act, act, act, ln_in_g, ln_in_b, conv_w, conv_b, conv_ln_g, conv_ln_b,
      w_sb, w_cv, w_out, ln_g, ln_b)


def kernel(x, ln_in_g, ln_in_b, w_in, w_sb_proj, conv_w, conv_b, conv_ln_g, conv_ln_b,
           w_cv_proj, w_out, ln_post_g, ln_post_b):
    bsz, seq, d = x.shape
    assert d == D_MODEL and w_in.shape == (DEPTH, D_MODEL, N_SPLITS * D_MODEL)
    assert seq % ATT_TILE == 0 and seq % OUT_ROWS == 0 and (bsz * seq) % PROJ_ROWS == 0
    x2d = x.reshape(bsz * seq, d)
    row = lambda p: p.reshape(1, D_MODEL)
    bf16 = lambda w: w.astype(jnp.bfloat16)

    w_in_bf16 = bf16(w_in[0])
    act, hn = _proj(x2d, row(ln_in_g), row(ln_in_b), w_in_bf16)
    u = _glu_proj(hn, w_in_bf16)
    a = _attention(act.reshape(bsz, seq, ACT_GROUPS * D_MODEL)).reshape(bsz * seq, D_MODEL)
    out = _out_stage(x2d, a, u, act, row(ln_in_g), row(ln_in_b), conv_w[0], row(conv_b[0]),
                     row(conv_ln_g[0]), row(conv_ln_b[0]), bf16(w_sb_proj[0]), bf16(w_cv_proj[0]),
                     bf16(w_out[0]), row(ln_post_g[0]), row(ln_post_b[0]), seq)
    return out.reshape(bsz, seq, d)
```

```python
import functools
import math

import jax
import jax.numpy as jnp
from jax import lax
from jax.experimental import pallas as pl
from jax.experimental.pallas import tpu as pltpu

D_MODEL = 1024
SB_HEADS = 16
SB_HEAD_DIM = 64
CV_KERNEL = 31
LN_EPS = 1e-5
DEPTH = 1
DEEPNORM_ALPHA = (2.0 * DEPTH) ** 0.25
N_SPLITS = 9
COL_Q, COL_K, COL_V, COL_ZS, COL_GLU_VALUE, COL_GLU_GATE, COL_ZC, COL_GS, COL_GC = range(N_SPLITS)

LOG2E = 1.4426950408889634
Q_SCALE = LOG2E / math.sqrt(SB_HEAD_DIM)

V7X_VMEM_LIMIT_BYTES = 56 * 1024 * 1024
LANES = 128
SUBLANES = 8

PROJ_ROWS = 1024
ATT_TILE = 256
ATT_HEADS = 4
MASKED_EXPONENT = -1e30
OUT_ROWS = 512
CONV_HALO = 32
CONV_STRIDE = 4
CONV_CHUNK = CONV_STRIDE * SUBLANES


def _layer_norm_f32(x, g, b):
    mu = jnp.mean(x, axis=-1, keepdims=True)
    xc = x - mu
    var = jnp.mean(xc * xc, axis=-1, keepdims=True)
    return xc * lax.rsqrt(var + LN_EPS) * g + b


def _sigmoid(x):
    return 1.0 / (1.0 + jnp.exp2(x * (-LOG2E)))


def _ln_kernel(x_ref, g_ref, b_ref, hn_ref):
    hn_ref[...] = _layer_norm_f32(x_ref[...], g_ref[...], b_ref[...]).astype(hn_ref.dtype)


def _ln_in(x2d, ln_g, ln_b):
    n_rows = x2d.shape[0]
    tm = PROJ_ROWS
    return pl.pallas_call(
        _ln_kernel,
        grid=(n_rows // tm,),
        in_specs=[pl.BlockSpec((tm, D_MODEL), lambda i: (i, 0)),
                  pl.BlockSpec((1, D_MODEL), lambda i: (0, 0)),
                  pl.BlockSpec((1, D_MODEL), lambda i: (0, 0))],
        out_specs=pl.BlockSpec((tm, D_MODEL), lambda i: (i, 0)),
        out_shape=jax.ShapeDtypeStruct((n_rows, D_MODEL), jnp.bfloat16),
        compiler_params=pltpu.CompilerParams(
            dimension_semantics=("arbitrary",),
            vmem_limit_bytes=V7X_VMEM_LIMIT_BYTES),
        name="ln_in",
    )(x2d, ln_g, ln_b)


def _proj_kernel(hn_ref, w_ref, out_ref, raw_ref, *, activation, n_groups):
    s = pl.program_id(0)

    @pl.when(s == 0)
    def _():
        raw_ref[...] = jnp.zeros_like(raw_ref)

    group = jnp.maximum(s - 1, 0) % n_groups
    out_ref[...] = activation(raw_ref[...], group).astype(out_ref.dtype)
    raw_ref[...] = jnp.dot(hn_ref[...], w_ref[...], preferred_element_type=jnp.float32)


def _proj(hn, w_in_bf16, first_col_block, col_block_stride, n_groups, activation, out_dtype, name,
          groups_per_block=1):
    n_rows = hn.shape[0]
    tm = PROJ_ROWS
    width = groups_per_block * D_MODEL
    n_work = (n_rows // tm) * n_groups
    cur = lambda s: jnp.minimum(s, n_work - 1)
    prev = lambda s: jnp.maximum(s - 1, 0)
    return pl.pallas_call(
        functools.partial(_proj_kernel, activation=activation, n_groups=n_groups),
        grid=(n_work + 1,),
        in_specs=[pl.BlockSpec((tm, D_MODEL), lambda s: (cur(s) // n_groups, 0)),
                  pl.BlockSpec((D_MODEL, width),
                               lambda s: (0, first_col_block + (cur(s) % n_groups) * col_block_stride))],
        out_specs=pl.BlockSpec((tm, D_MODEL), lambda s: (prev(s) // n_groups, prev(s) % n_groups)),
        out_shape=jax.ShapeDtypeStruct((n_rows, n_groups * D_MODEL), out_dtype),
        scratch_shapes=[pltpu.VMEM((tm, width), jnp.float32)],
        compiler_params=pltpu.CompilerParams(
            dimension_semantics=("arbitrary",),
            vmem_limit_bytes=V7X_VMEM_LIMIT_BYTES),
        name=name,
    )(hn, w_in_bf16)


def _act_qkv(raw, group):
    return raw * jnp.where(group == 0, Q_SCALE, 1.0).astype(jnp.float32)


def _act_silu(raw, group):
    return raw * _sigmoid(raw)


def _act_sigmoid(raw, group):
    return _sigmoid(raw)


def _act_glu(raw, group):
    return raw[:, :D_MODEL] * _sigmoid(raw[:, D_MODEL:])


def _softplus2_and_logsig2(l):
    g = jnp.log2(1.0 + jnp.exp2(jnp.minimum(l, -l)))
    sp = jnp.maximum(l, 0.0) + g
    return sp, l - sp


def _attention_kernel(q_ref, k_ref, v_ref, zs_ref, o_ref,
                      suffix_ref, qt_ref, vt_ref, acc_ref, rs_ref, z_ref, sp_ref, sp0_ref, own_ref):
    seq = q_ref.shape[0]
    t = ATT_TILE
    nh = ATT_HEADS
    n_blocks = seq // t
    n_diag_pairs = n_blocks // 2
    n_pairs = n_diag_pairs + n_blocks * (n_blocks - 1) // 4

    key = lax.broadcasted_iota(jnp.int32, (t, t), 0)
    qry = lax.broadcasted_iota(jnp.int32, (t, t), 1)
    suffix_ref[...] = (qry > key).astype(jnp.bfloat16)
    diag_mask = jnp.concatenate([key < qry] * nh, axis=1)
    dim_head = lax.broadcasted_iota(jnp.int32, (nh * SB_HEAD_DIM, t), 0) // SB_HEAD_DIM

    def per_head_t(x):
        xt = x.T
        zero = jnp.zeros_like(xt)
        return jnp.concatenate([jnp.where(dim_head == h, xt, zero) for h in range(nh)], axis=1)

    for j in range(n_blocks):
        qt_ref[j] = per_head_t(q_ref[j * t:(j + 1) * t, :])
        vt_ref[j] = per_head_t(v_ref[j * t:(j + 1) * t, :])
    acc_ref[...] = jnp.zeros_like(acc_ref)
    z_ref[0] = jnp.full_like(z_ref[0], MASKED_EXPONENT)

    def tile_of(p, c):
        n = 2 * (p - n_diag_pairs) + c
        qi = 1 + sum(jnp.where(n >= k * (k + 1) // 2, 1, 0) for k in range(1, n_blocks - 1))
        j = qi - 1 - (n - qi * (qi - 1) // 2)
        on_diag = p < n_diag_pairs
        return jnp.where(on_diag, 2 * p + c, qi), jnp.where(on_diag, 2 * p + c, j)

    def logits(p):
        ls = []
        for c in range(2):
            qi, j = tile_of(p, c)
            kt = k_ref[pl.ds(pl.multiple_of(j * t, t), t), :]
            ls.append(jnp.dot(kt, qt_ref[qi], preferred_element_type=jnp.float32))
        return ls

    def exponent_stage(p, ls, diag):
        for c in range(2):
            qi, j = tile_of(p, c)
            sp, own = _softplus2_and_logsig2(ls[c])
            if diag:
                sp = jnp.where(diag_mask, sp, 0.0)
            tot = jnp.dot(suffix_ref[...], sp.astype(jnp.bfloat16), preferred_element_type=jnp.float32)
            if not diag:
                tot = (tot.reshape(t // SUBLANES, SUBLANES, nh * t) + rs_ref[qi][None]).reshape(t, nh * t)
            rs_ref[qi] = jnp.broadcast_to(tot[0:1] + sp[0:1], (SUBLANES, nh * t))
            z = own - tot
            if diag:
                z = jnp.where(diag_mask, z, MASKED_EXPONENT)
            z_ref[0, c] = z

    def weight_stage(p, slot):
        for c in range(2):
            qi, j = tile_of(p, c)
            wb = jnp.exp2(z_ref[slot, c]).astype(jnp.bfloat16)
            w_stack = jnp.concatenate([wb[:, h * t:(h + 1) * t] for h in range(nh)], axis=0)
            acc_ref[qi] += jnp.dot(vt_ref[j], w_stack, preferred_element_type=jnp.float32)

    def softplus_stage(p, slot):
        for c, l in enumerate(logits(p)):
            sp, own = _softplus2_and_logsig2(l)
            sp_ref[slot, c] = sp.astype(jnp.bfloat16)
            sp0_ref[slot, c] = sp[0:SUBLANES]
            own_ref[slot, c] = own

    def later_sum_stage(p, slot):
        for c in range(2):
            qi, _ = tile_of(p, c)
            tot = jnp.dot(suffix_ref[...], sp_ref[slot, c], preferred_element_type=jnp.float32)
            tot = (tot.reshape(t // SUBLANES, SUBLANES, nh * t) + rs_ref[qi][None]).reshape(t, nh * t)
            rs_ref[qi] = jnp.broadcast_to(tot[0:1] + sp0_ref[slot, c][0:1], (SUBLANES, nh * t))
            z_ref[slot, c] = own_ref[slot, c] - tot

    def diag_pair(p, _):
        ls = logits(p)
        weight_stage(jnp.maximum(p - 1, 0), 0)
        exponent_stage(p, ls, True)
        return 0

    first = n_diag_pairs

    def two_full_pairs(i, _):
        p = first + 2 + 2 * i
        weight_stage(p - 2, 0)
        softplus_stage(p, 0)
        later_sum_stage(p - 1, 1)
        weight_stage(p - 1, 1)
        softplus_stage(p + 1, 1)
        later_sum_stage(p, 0)
        return 0

    lax.fori_loop(0, first, diag_pair, 0)
    weight_stage(first - 1, 0)
    softplus_stage(first, 0)
    softplus_stage(first + 1, 1)
    later_sum_stage(first, 0)
    lax.fori_loop(0, (n_pairs - first - 2) // 2, two_full_pairs, 0)
    later_sum_stage(n_pairs - 1, 1)
    weight_stage(n_pairs - 2, 0)
    weight_stage(n_pairs - 1, 1)

    for qi in range(n_blocks):
        gate = zs_ref[qi * t:(qi + 1) * t, :].astype(jnp.float32)
        o_ref[qi * t:(qi + 1) * t, :] = (acc_ref[qi].T * gate).astype(o_ref.dtype)


def _attention(qkv3, zz3):
    bsz, seq, _ = qkv3.shape
    lanes = ATT_HEADS * SB_HEAD_DIM
    per_group = D_MODEL // lanes
    n_blocks = seq // ATT_TILE
    assert n_blocks % 4 == 0
    group_spec = lambda g: pl.BlockSpec((None, seq, lanes), lambda b, p: (b, 0, g * per_group + p))
    return pl.pallas_call(
        _attention_kernel,
        grid=(bsz, per_group),
        in_specs=[group_spec(0), group_spec(1), group_spec(2), group_spec(0)],
        out_specs=pl.BlockSpec((None, seq, lanes), lambda b, p: (b, 0, p)),
        out_shape=jax.ShapeDtypeStruct((bsz, seq, D_MODEL), jnp.bfloat16),
        scratch_shapes=[pltpu.VMEM((ATT_TILE, ATT_TILE), jnp.bfloat16),
                        pltpu.VMEM((n_blocks, lanes, ATT_HEADS * ATT_TILE), jnp.bfloat16),
                        pltpu.VMEM((n_blocks, lanes, ATT_HEADS * ATT_TILE), jnp.bfloat16),
                        pltpu.VMEM((n_blocks, lanes, ATT_TILE), jnp.float32),
                        pltpu.VMEM((n_blocks, SUBLANES, ATT_HEADS * ATT_TILE), jnp.float32),
                        pltpu.VMEM((2, 2, ATT_TILE, ATT_HEADS * ATT_TILE), jnp.float32),
                        pltpu.VMEM((2, 2, ATT_TILE, ATT_HEADS * ATT_TILE), jnp.bfloat16),
                        pltpu.VMEM((2, 2, SUBLANES, ATT_HEADS * ATT_TILE), jnp.float32),
                        pltpu.VMEM((2, 2, ATT_TILE, ATT_HEADS * ATT_TILE), jnp.float32)],
        compiler_params=pltpu.CompilerParams(
            dimension_semantics=("arbitrary", "arbitrary"),
            vmem_limit_bytes=V7X_VMEM_LIMIT_BYTES),
        name="sb_attention",
    )(qkv3, qkv3, qkv3, zz3)


def _out_kernel(x_ref, a_ref, u_ref, halo_ref, zc_ref, gs_ref, gc_ref,
                ln_in_g_ref, ln_in_b_ref, cw_ref, cb_ref, cg_ref, cbeta_ref,
                w_sb_ref, w_cv_ref, w_out_ref, ln_g_ref, ln_b_ref,
                o_ref, ext_ref, conv_ref, wb_ref, *, tiles_per_seq):
    i = pl.program_id(0)
    tm = x_ref.shape[0]
    n_slabs = D_MODEL // LANES

    @pl.when(i == 0)
    def _():
        for tap in range(CV_KERNEL):
            for s in range(n_slabs):
                wb_ref[s, tap * SUBLANES:(tap + 1) * SUBLANES, :] = jnp.broadcast_to(
                    cw_ref[tap:tap + 1, s * LANES:(s + 1) * LANES], (SUBLANES, LANES))

    first = (i % tiles_per_seq) == 0
    for s in range(n_slabs):
        halo = halo_ref[:, s * LANES:(s + 1) * LANES]
        ext_ref[s, 0:CONV_HALO, :] = jnp.where(first, jnp.zeros_like(halo), halo)
        ext_ref[s, CONV_HALO:, :] = u_ref[:, s * LANES:(s + 1) * LANES]

    base = CONV_HALO - (CV_KERNEL - 1)

    def conv_chunk(c, _):
        r0 = pl.multiple_of(c * CONV_CHUNK, CONV_CHUNK)
        for s in range(n_slabs):
            win = [ext_ref[s, pl.ds(r0 + base + m, SUBLANES, stride=CONV_STRIDE), :]
                   for m in range(CONV_STRIDE + CV_KERNEL - 1)]
            acc = [jnp.zeros((SUBLANES, LANES), jnp.float32) for _ in range(CONV_STRIDE)]
            for tap in range(CV_KERNEL):
                w = wb_ref[s, tap * SUBLANES:(tap + 1) * SUBLANES, :]
                for ph in range(CONV_STRIDE):
                    acc[ph] = acc[ph] + w * win[ph + tap]
            for ph in range(CONV_STRIDE):
                conv_ref[s, pl.ds(r0 + ph, SUBLANES, stride=CONV_STRIDE), :] = acc[ph]
        return 0

    lax.fori_loop(0, tm // CONV_CHUNK, conv_chunk, 0)

    conv = jnp.concatenate([conv_ref[s] for s in range(n_slabs)], axis=-1) + cb_ref[...]
    cn = _layer_norm_f32(conv, cg_ref[...], cbeta_ref[...])
    ucv = cn * _sigmoid(cn) * zc_ref[...].astype(jnp.float32)
    y_cv = jnp.dot(ucv.astype(jnp.bfloat16), w_cv_ref[...], preferred_element_type=jnp.float32)
    y_sb = jnp.dot(a_ref[...], w_sb_ref[...], preferred_element_type=jnp.float32)
    merged = (gs_ref[...].astype(jnp.float32) * y_sb + gc_ref[...].astype(jnp.float32) * y_cv)
    proj = jnp.dot(merged.astype(jnp.bfloat16), w_out_ref[...], preferred_element_type=jnp.float32)
    h0 = _layer_norm_f32(x_ref[...], ln_in_g_ref[...], ln_in_b_ref[...])
    o_ref[...] = _layer_norm_f32(DEEPNORM_ALPHA * h0 + proj, ln_g_ref[...], ln_b_ref[...])


def _out_stage(x2d, a, u, zz, gg, ln_in_g, ln_in_b, conv_w, conv_b, conv_ln_g, conv_ln_b,
               w_sb, w_cv, w_out, ln_g, ln_b, seq):
    n_rows = x2d.shape[0]
    tm = OUT_ROWS
    halo_per_tile = tm // CONV_HALO
    row_spec = pl.BlockSpec((tm, D_MODEL), lambda i: (i, 0))
    halo_spec = pl.BlockSpec((CONV_HALO, D_MODEL),
                             lambda i: (jnp.maximum(i * halo_per_tile - 1, 0), 0))
    act_spec = lambda g: pl.BlockSpec((tm, D_MODEL), lambda i: (i, g))
    vec_spec = pl.BlockSpec((1, D_MODEL), lambda i: (0, 0))
    mat_spec = pl.BlockSpec((D_MODEL, D_MODEL), lambda i: (0, 0))
    return pl.pallas_call(
        functools.partial(_out_kernel, tiles_per_seq=seq // tm),
        grid=(n_rows // tm,),
        in_specs=[row_spec, row_spec, row_spec, halo_spec,
                  act_spec(1), act_spec(0), act_spec(1),
                  vec_spec, vec_spec,
                  pl.BlockSpec((CV_KERNEL, D_MODEL), lambda i: (0, 0)),
                  vec_spec, vec_spec, vec_spec,
                  mat_spec, mat_spec, mat_spec, vec_spec, vec_spec],
        out_specs=row_spec,
        out_shape=jax.ShapeDtypeStruct((n_rows, D_MODEL), jnp.float32),
        scratch_shapes=[pltpu.VMEM((D_MODEL // LANES, tm + CONV_HALO, LANES), jnp.float32),
                        pltpu.VMEM((D_MODEL // LANES, tm, LANES), jnp.float32),
                        pltpu.VMEM((D_MODEL // LANES, CV_KERNEL * SUBLANES, LANES), jnp.float32)],
        compiler_params=pltpu.CompilerParams(
            dimension_semantics=("arbitrary",),
            vmem_limit_bytes=V7X_VMEM_LIMIT_BYTES),
        name="out_stage",
    )(x2d, a, u, u, zz, gg, gg, ln_in_g, ln_in_b, conv_w, conv_b, conv_ln_g, conv_ln_b,
      w_sb, w_cv, w_out, ln_g, ln_b)


def kernel(x, ln_in_g, ln_in_b, w_in, w_sb_proj, conv_w, conv_b, conv_ln_g, conv_ln_b,
           w_cv_proj, w_out, ln_post_g, ln_post_b):
    bsz, seq, d = x.shape
    assert d == D_MODEL and w_in.shape == (DEPTH, D_MODEL, N_SPLITS * D_MODEL)
    assert seq % ATT_TILE == 0 and seq % OUT_ROWS == 0 and (bsz * seq) % PROJ_ROWS == 0
    x2d = x.reshape(bsz * seq, d)
    row = lambda p: p.reshape(1, D_MODEL)
    bf16 = lambda w: w.astype(jnp.bfloat16)

    w_in_bf16 = bf16(w_in[0])
    hn = _ln_in(x2d, row(ln_in_g), row(ln_in_b))
    qkv = _proj(hn, w_in_bf16, COL_Q, 1, 3, _act_qkv, jnp.bfloat16, "qkv_proj")
    zz = _proj(hn, w_in_bf16, COL_ZS, COL_ZC - COL_ZS, 2, _act_silu, jnp.bfloat16, "silu_proj")
    gg = _proj(hn, w_in_bf16, COL_GS, 1, 2, _act_sigmoid, jnp.bfloat16, "gate_proj")
    u = _proj(hn, w_in_bf16, COL_GLU_VALUE // 2, 1, 1, _act_glu, jnp.float32, "glu_proj", groups_per_block=2)
    a = _attention(qkv.reshape(bsz, seq, 3 * D_MODEL), zz.reshape(bsz, seq, 2 * D_MODEL))
    out = _out_stage(x2d, a.reshape(bsz * seq, D_MODEL), u, zz, gg, row(ln_in_g), row(ln_in_b),
                     conv_w[0], row(conv_b[0]), row(conv_ln_g[0]), row(conv_ln_b[0]),
                     bf16(w_sb_proj[0]), bf16(w_cv_proj[0]), bf16(w_out[0]),
                     row(ln_post_g[0]), row(ln_post_b[0]), seq)
    return out.reshape(bsz, seq, d)
```

```python
import functools
import math

import jax
import jax.numpy as jnp
from jax import lax
from jax.experimental import pallas as pl
from jax.experimental.pallas import tpu as pltpu

D_MODEL = 1024
SB_HEADS = 16
SB_HEAD_DIM = 64
CV_KERNEL = 31
LN_EPS = 1e-5
DEPTH = 1
DEEPNORM_ALPHA = (2.0 * DEPTH) ** 0.25
N_SPLITS = 9
COL_Q, COL_K, COL_V, COL_ZS, COL_GLU_VALUE, COL_GLU_GATE, COL_ZC, COL_GS, COL_GC = range(N_SPLITS)

LOG2E = 1.4426950408889634
Q_SCALE = LOG2E / math.sqrt(SB_HEAD_DIM)

V7X_VMEM_LIMIT_BYTES = 56 * 1024 * 1024
LANES = 128
SUBLANES = 8

PROJ_ROWS = 1024
ATT_TILE = 256
ATT_HEADS = 4
MASKED_EXPONENT = -1e30
OUT_ROWS = 512
CONV_HALO = 32
CONV_STRIDE = 4
CONV_CHUNK = CONV_STRIDE * SUBLANES


def _layer_norm_f32(x, g, b):
    mu = jnp.mean(x, axis=-1, keepdims=True)
    xc = x - mu
    var = jnp.mean(xc * xc, axis=-1, keepdims=True)
    return xc * lax.rsqrt(var + LN_EPS) * g + b


def _sigmoid(x):
    return 1.0 / (1.0 + jnp.exp2(x * (-LOG2E)))


def _ln_kernel(x_ref, g_ref, b_ref, hn_ref):
    hn_ref[...] = _layer_norm_f32(x_ref[...], g_ref[...], b_ref[...]).astype(hn_ref.dtype)


def _ln_in(x2d, ln_g, ln_b):
    n_rows = x2d.shape[0]
    tm = PROJ_ROWS
    return pl.pallas_call(
        _ln_kernel,
        grid=(n_rows // tm,),
        in_specs=[pl.BlockSpec((tm, D_MODEL), lambda i: (i, 0)),
                  pl.BlockSpec((1, D_MODEL), lambda i: (0, 0)),
                  pl.BlockSpec((1, D_MODEL), lambda i: (0, 0))],
        out_specs=pl.BlockSpec((tm, D_MODEL), lambda i: (i, 0)),
        out_shape=jax.ShapeDtypeStruct((n_rows, D_MODEL), jnp.bfloat16),
        compiler_params=pltpu.CompilerParams(
            dimension_semantics=("arbitrary",),
            vmem_limit_bytes=V7X_VMEM_LIMIT_BYTES),
        name="ln_in",
    )(x2d, ln_g, ln_b)


def _proj_kernel(hn_ref, w_ref, out_ref, raw_ref, *, activation, n_groups):
    s = pl.program_id(0)

    @pl.when(s == 0)
    def _():
        raw_ref[...] = jnp.zeros_like(raw_ref)

    group = jnp.maximum(s - 1, 0) % n_groups
    out_ref[...] = activation(raw_ref[...], group).astype(out_ref.dtype)
    raw_ref[...] = jnp.dot(hn_ref[...], w_ref[...], preferred_element_type=jnp.float32)


def _proj(hn, w_in_bf16, first_col_block, col_block_stride, n_groups, activation, out_dtype, name,
          groups_per_block=1):
    n_rows = hn.shape[0]
    tm = PROJ_ROWS
    width = groups_per_block * D_MODEL
    n_work = (n_rows // tm) * n_groups
    cur = lambda s: jnp.minimum(s, n_work - 1)
    prev = lambda s: jnp.maximum(s - 1, 0)
    return pl.pallas_call(
        functools.partial(_proj_kernel, activation=activation, n_groups=n_groups),
        grid=(n_work + 1,),
        in_specs=[pl.BlockSpec((tm, D_MODEL), lambda s: (cur(s) // n_groups, 0)),
                  pl.BlockSpec((D_MODEL, width),
                               lambda s: (0, first_col_block + (cur(s) % n_groups) * col_block_stride))],
        out_specs=pl.BlockSpec((tm, D_MODEL), lambda s: (prev(s) // n_groups, prev(s) % n_groups)),
        out_shape=jax.ShapeDtypeStruct((n_rows, n_groups * D_MODEL), out_dtype),
        scratch_shapes=[pltpu.VMEM((tm, width), jnp.float32)],
        compiler_params=pltpu.CompilerParams(
            dimension_semantics=("arbitrary",),
            vmem_limit_bytes=V7X_VMEM_LIMIT_BYTES),
        name=name,
    )(hn, w_in_bf16)


def _act_qkv(raw, group):
    return raw * jnp.where(group == 0, Q_SCALE, 1.0).astype(jnp.float32)


def _act_silu(raw, group):
    return raw * _sigmoid(raw)


def _act_sigmoid(raw, group):
    return _sigmoid(raw)


def _act_glu(raw, group):
    return raw[:, :D_MODEL] * _sigmoid(raw[:, D_MODEL:])


def _softplus2_and_logsig2(l):
    g = jnp.log2(1.0 + jnp.exp2(jnp.minimum(l, -l)))
    sp = jnp.maximum(l, 0.0) + g
    return sp, l - sp


def _attention_kernel(q_ref, k_ref, v_ref, zs_ref, o_ref,
                      suffix_ref, qt_ref, vt_ref, acc_ref, rs_ref, z_ref):
    seq = q_ref.shape[0]
    t = ATT_TILE
    nh = ATT_HEADS
    n_blocks = seq // t
    n_diag_pairs = n_blocks // 2
    n_pairs = n_diag_pairs + n_blocks * (n_blocks - 1) // 4

    key = lax.broadcasted_iota(jnp.int32, (t, t), 0)
    qry = lax.broadcasted_iota(jnp.int32, (t, t), 1)
    suffix_ref[...] = (qry > key).astype(jnp.bfloat16)
    diag_mask = jnp.concatenate([key < qry] * nh, axis=1)
    dim_head = lax.broadcasted_iota(jnp.int32, (nh * SB_HEAD_DIM, t), 0) // SB_HEAD_DIM

    def per_head_t(x):
        xt = x.T
        zero = jnp.zeros_like(xt)
        return jnp.concatenate([jnp.where(dim_head == h, xt, zero) for h in range(nh)], axis=1)

    for j in range(n_blocks):
        qt_ref[j] = per_head_t(q_ref[j * t:(j + 1) * t, :])
        vt_ref[j] = per_head_t(v_ref[j * t:(j + 1) * t, :])
    acc_ref[...] = jnp.zeros_like(acc_ref)
    z_ref[...] = jnp.full_like(z_ref, MASKED_EXPONENT)

    def tile_of(p, c):
        n = 2 * (p - n_diag_pairs) + c
        qi = 1 + sum(jnp.where(n >= k * (k + 1) // 2, 1, 0) for k in range(1, n_blocks - 1))
        j = qi - 1 - (n - qi * (qi - 1) // 2)
        on_diag = p < n_diag_pairs
        return jnp.where(on_diag, 2 * p + c, qi), jnp.where(on_diag, 2 * p + c, j)

    def logits(p):
        ls = []
        for c in range(2):
            qi, j = tile_of(p, c)
            kt = k_ref[pl.ds(pl.multiple_of(j * t, t), t), :]
            ls.append(jnp.dot(kt, qt_ref[qi], preferred_element_type=jnp.float32))
        return ls

    def exponent_stage(p, ls, diag):
        for c in range(2):
            qi, j = tile_of(p, c)
            sp, own = _softplus2_and_logsig2(ls[c])
            if diag:
                sp = jnp.where(diag_mask, sp, 0.0)
            tot = jnp.dot(suffix_ref[...], sp.astype(jnp.bfloat16), preferred_element_type=jnp.float32)
            if not diag:
                tot = (tot.reshape(t // SUBLANES, SUBLANES, nh * t) + rs_ref[qi][None]).reshape(t, nh * t)
            rs_ref[qi] = jnp.broadcast_to(tot[0:1] + sp[0:1], (SUBLANES, nh * t))
            z = own - tot
            if diag:
                z = jnp.where(diag_mask, z, MASKED_EXPONENT)
            z_ref[c] = z

    def weight_stage(p):
        for c in range(2):
            qi, j = tile_of(p, c)
            wb = jnp.exp2(z_ref[c]).astype(jnp.bfloat16)
            w_stack = jnp.concatenate([wb[:, h * t:(h + 1) * t] for h in range(nh)], axis=0)
            acc_ref[qi] += jnp.dot(vt_ref[j], w_stack, preferred_element_type=jnp.float32)

    def diag_pair(p, _):
        ls = logits(p)
        weight_stage(jnp.maximum(p - 1, 0))
        exponent_stage(p, ls, True)
        return 0

    def full_pair(p, _):
        ls = logits(p)
        weight_stage(p - 1)
        exponent_stage(p, ls, False)
        return 0

    lax.fori_loop(0, n_diag_pairs, diag_pair, 0)
    lax.fori_loop(n_diag_pairs, n_pairs, full_pair, 0)
    weight_stage(n_pairs - 1)

    for qi in range(n_blocks):
        gate = zs_ref[qi * t:(qi + 1) * t, :].astype(jnp.float32)
        o_ref[qi * t:(qi + 1) * t, :] = (acc_ref[qi].T * gate).astype(o_ref.dtype)


def _attention(qkv3, zz3):
    bsz, seq, _ = qkv3.shape
    lanes = ATT_HEADS * SB_HEAD_DIM
    per_group = D_MODEL // lanes
    n_blocks = seq // ATT_TILE
    assert n_blocks % 4 == 0
    group_spec = lambda g: pl.BlockSpec((None, seq, lanes), lambda b, p: (b, 0, g * per_group + p))
    return pl.pallas_call(
        _attention_kernel,
        grid=(bsz, per_group),
        in_specs=[group_spec(0), group_spec(1), group_spec(2), group_spec(0)],
        out_specs=pl.BlockSpec((None, seq, lanes), lambda b, p: (b, 0, p)),
        out_shape=jax.ShapeDtypeStruct((bsz, seq, D_MODEL), jnp.bfloat16),
        scratch_shapes=[pltpu.VMEM((ATT_TILE, ATT_TILE), jnp.bfloat16),
                        pltpu.VMEM((n_blocks, lanes, ATT_HEADS * ATT_TILE), jnp.bfloat16),
                        pltpu.VMEM((n_blocks, lanes, ATT_HEADS * ATT_TILE), jnp.bfloat16),
                        pltpu.VMEM((n_blocks, lanes, ATT_TILE), jnp.float32),
                        pltpu.VMEM((n_blocks, SUBLANES, ATT_HEADS * ATT_TILE), jnp.float32),
                        pltpu.VMEM((2, ATT_TILE, ATT_HEADS * ATT_TILE), jnp.float32)],
        compiler_params=pltpu.CompilerParams(
            dimension_semantics=("arbitrary", "arbitrary"),
            vmem_limit_bytes=V7X_VMEM_LIMIT_BYTES),
        name="sb_attention",
    )(qkv3, qkv3, qkv3, zz3)


def _out_kernel(x_ref, a_ref, u_ref, halo_ref, zc_ref, gs_ref, gc_ref,
                ln_in_g_ref, ln_in_b_ref, cw_ref, cb_ref, cg_ref, cbeta_ref,
                w_sb_ref, w_cv_ref, w_out_ref, ln_g_ref, ln_b_ref,
                o_ref, ext_ref, conv_ref, wb_ref, *, tiles_per_seq):
    i = pl.program_id(0)
    tm = x_ref.shape[0]
    n_slabs = D_MODEL // LANES

    @pl.when(i == 0)
    def _():
        for tap in range(CV_KERNEL):
            for s in range(n_slabs):
                wb_ref[s, tap * SUBLANES:(tap + 1) * SUBLANES, :] = jnp.broadcast_to(
                    cw_ref[tap:tap + 1, s * LANES:(s + 1) * LANES], (SUBLANES, LANES))

    first = (i % tiles_per_seq) == 0
    for s in range(n_slabs):
        halo = halo_ref[:, s * LANES:(s + 1) * LANES]
        ext_ref[s, 0:CONV_HALO, :] = jnp.where(first, jnp.zeros_like(halo), halo)
        ext_ref[s, CONV_HALO:, :] = u_ref[:, s * LANES:(s + 1) * LANES]

    base = CONV_HALO - (CV_KERNEL - 1)

    def conv_chunk(c, _):
        r0 = pl.multiple_of(c * CONV_CHUNK, CONV_CHUNK)
        for s in range(n_slabs):
            win = [ext_ref[s, pl.ds(r0 + base + m, SUBLANES, stride=CONV_STRIDE), :]
                   for m in range(CONV_STRIDE + CV_KERNEL - 1)]
            acc = [jnp.zeros((SUBLANES, LANES), jnp.float32) for _ in range(CONV_STRIDE)]
            for tap in range(CV_KERNEL):
                w = wb_ref[s, tap * SUBLANES:(tap + 1) * SUBLANES, :]
                for ph in range(CONV_STRIDE):
                    acc[ph] = acc[ph] + w * win[ph + tap]
            for ph in range(CONV_STRIDE):
                conv_ref[s, pl.ds(r0 + ph, SUBLANES, stride=CONV_STRIDE), :] = acc[ph]
        return 0

    lax.fori_loop(0, tm // CONV_CHUNK, conv_chunk, 0)

    conv = jnp.concatenate([conv_ref[s] for s in range(n_slabs)], axis=-1) + cb_ref[...]
    cn = _layer_norm_f32(conv, cg_ref[...], cbeta_ref[...])
    ucv = cn * _sigmoid(cn) * zc_ref[...].astype(jnp.float32)
    y_cv = jnp.dot(ucv.astype(jnp.bfloat16), w_cv_ref[...], preferred_element_type=jnp.float32)
    y_sb = jnp.dot(a_ref[...], w_sb_ref[...], preferred_element_type=jnp.float32)
    merged = (gs_ref[...].astype(jnp.float32) * y_sb + gc_ref[...].astype(jnp.float32) * y_cv)
    proj = jnp.dot(merged.astype(jnp.bfloat16), w_out_ref[...], preferred_element_type=jnp.float32)
    h0 = _layer_norm_f32(x_ref[...], ln_in_g_ref[...], ln_in_b_ref[...])
    o_ref[...] = _layer_norm_f32(DEEPNORM_ALPHA * h0 + proj, ln_g_ref[...], ln_b_ref[...])


def _out_stage(x2d, a, u, zz, gg, ln_in_g, ln_in_b, conv_w, conv_b, conv_ln_g, conv_ln_b,
               w_sb, w_cv, w_out, ln_g, ln_b, seq):
    n_rows = x2d.shape[0]
    tm = OUT_ROWS
    halo_per_tile = tm // CONV_HALO
    row_spec = pl.BlockSpec((tm, D_MODEL), lambda i: (i, 0))
    halo_spec = pl.BlockSpec((CONV_HALO, D_MODEL),
                             lambda i: (jnp.maximum(i * halo_per_tile - 1, 0), 0))
    act_spec = lambda g: pl.BlockSpec((tm, D_MODEL), lambda i: (i, g))
    vec_spec = pl.BlockSpec((1, D_MODEL), lambda i: (0, 0))
    mat_spec = pl.BlockSpec((D_MODEL, D_MODEL), lambda i: (0, 0))
    return pl.pallas_call(
        functools.partial(_out_kernel, tiles_per_seq=seq // tm),
        grid=(n_rows // tm,),
        in_specs=[row_spec, row_spec, row_spec, halo_spec,
                  act_spec(1), act_spec(0), act_spec(1),
                  vec_spec, vec_spec,
                  pl.BlockSpec((CV_KERNEL, D_MODEL), lambda i: (0, 0)),
                  vec_spec, vec_spec, vec_spec,
                  mat_spec, mat_spec, mat_spec, vec_spec, vec_spec],
        out_specs=row_spec,
        out_shape=jax.ShapeDtypeStruct((n_rows, D_MODEL), jnp.float32),
        scratch_shapes=[pltpu.VMEM((D_MODEL // LANES, tm + CONV_HALO, LANES), jnp.float32),
                        pltpu.VMEM((D_MODEL // LANES, tm, LANES), jnp.float32),
                        pltpu.VMEM((D_MODEL // LANES, CV_KERNEL * SUBLANES, LANES), jnp.float32)],
        compiler_params=pltpu.CompilerParams(
            dimension_semantics=("arbitrary",),
            vmem_limit_bytes=V7X_VMEM_LIMIT_BYTES),
        name="out_stage",
    )(x2d, a, u, u, zz, gg, gg, ln_in_g, ln_in_b, conv_w, conv_b, conv_ln_g, conv_ln_b,
      w_sb, w_cv, w_out, ln_g, ln_b)


def kernel(x, ln_in_g, ln_in_b, w_in, w_sb_proj, conv_w, conv_b, conv_ln_g, conv_ln_b,
           w_cv_proj, w_out, ln_post_g, ln_post_b):
    bsz, seq, d = x.shape
    assert d == D_MODEL and w_in.shape == (DEPTH, D_MODEL, N_SPLITS * D_MODEL)
    assert seq % ATT_TILE == 0 and seq % OUT_ROWS == 0 and (bsz * seq) % PROJ_ROWS == 0
    x2d = x.reshape(bsz * seq, d)
    row = lambda p: p.reshape(1, D_MODEL)
    bf16 = lambda w: w.astype(jnp.bfloat16)

    w_in_bf16 = bf16(w_in[0])
    hn = _ln_in(x2d, row(ln_in_g), row(ln_in_b))
    qkv = _proj(hn, w_in_bf16, COL_Q, 1, 3, _act_qkv, jnp.bfloat16, "qkv_proj")
    zz = _proj(hn, w_in_bf16, COL_ZS, COL_ZC - COL_ZS, 2, _act_silu, jnp.bfloat16, "silu_proj")
    gg = _proj(hn, w_in_bf16, COL_GS, 1, 2, _act_sigmoid, jnp.bfloat16, "gate_proj")
    u = _proj(hn, w_in_bf16, COL_GLU_VALUE // 2, 1, 1, _act_glu, jnp.float32, "glu_proj", groups_per_block=2)
    a = _attention(qkv.reshape(bsz, seq, 3 * D_MODEL), zz.reshape(bsz, seq, 2 * D_MODEL))
    out = _out_stage(x2d, a.reshape(bsz * seq, D_MODEL), u, zz, gg, row(ln_in_g), row(ln_in_b),
                     conv_w[0], row(conv_b[0]), row(conv_ln_g[0]), row(conv_ln_b[0]),
                     bf16(w_sb_proj[0]), bf16(w_cv_proj[0]), bf16(w_out[0]),
                     row(ln_post_g[0]), row(ln_post_b[0]), seq)
    return out.reshape(bsz, seq, d)
```

```python
import functools
import math

import jax
import jax.numpy as jnp
from jax import lax
from jax.experimental import pallas as pl
from jax.experimental.pallas import tpu as pltpu

D_MODEL = 1024
SB_HEADS = 16
SB_HEAD_DIM = 64
CV_KERNEL = 31
LN_EPS = 1e-5
DEPTH = 1
DEEPNORM_ALPHA = (2.0 * DEPTH) ** 0.25
N_SPLITS = 9
COL_Q, COL_K, COL_V, COL_ZS, COL_GLU_VALUE, COL_GLU_GATE, COL_ZC, COL_GS, COL_GC = range(N_SPLITS)

LOG2E = 1.4426950408889634
Q_SCALE = LOG2E / math.sqrt(SB_HEAD_DIM)

V7X_VMEM_LIMIT_BYTES = 56 * 1024 * 1024
LANES = 128
SUBLANES = 8

PROJ_ROWS = 2048
ATT_TILE = 256
ATT_HEADS = 4
MASKED_EXPONENT = -1e30
OUT_ROWS = 512
CONV_HALO = 32
CONV_STRIDE = 4
CONV_CHUNK = CONV_STRIDE * SUBLANES


def _layer_norm_f32(x, g, b):
    mu = jnp.mean(x, axis=-1, keepdims=True)
    xc = x - mu
    var = jnp.mean(xc * xc, axis=-1, keepdims=True)
    return xc * lax.rsqrt(var + LN_EPS) * g + b


def _sigmoid(x):
    return 1.0 / (1.0 + jnp.exp2(x * (-LOG2E)))


def _ln_kernel(x_ref, g_ref, b_ref, hn_ref):
    hn_ref[...] = _layer_norm_f32(x_ref[...], g_ref[...], b_ref[...]).astype(hn_ref.dtype)


def _ln_in(x2d, ln_g, ln_b):
    n_rows = x2d.shape[0]
    tm = PROJ_ROWS
    return pl.pallas_call(
        _ln_kernel,
        grid=(n_rows // tm,),
        in_specs=[pl.BlockSpec((tm, D_MODEL), lambda i: (i, 0)),
                  pl.BlockSpec((1, D_MODEL), lambda i: (0, 0)),
                  pl.BlockSpec((1, D_MODEL), lambda i: (0, 0))],
        out_specs=pl.BlockSpec((tm, D_MODEL), lambda i: (i, 0)),
        out_shape=jax.ShapeDtypeStruct((n_rows, D_MODEL), jnp.bfloat16),
        compiler_params=pltpu.CompilerParams(
            dimension_semantics=("arbitrary",),
            vmem_limit_bytes=V7X_VMEM_LIMIT_BYTES),
        name="ln_in",
    )(x2d, ln_g, ln_b)


def _proj_kernel(hn_ref, w_ref, out_ref, raw_ref, *, activation, n_groups):
    s = pl.program_id(0)

    @pl.when(s == 0)
    def _():
        raw_ref[...] = jnp.zeros_like(raw_ref)

    group = jnp.maximum(s - 1, 0) % n_groups
    out_ref[...] = activation(raw_ref[...], group).astype(out_ref.dtype)
    raw_ref[...] = jnp.dot(hn_ref[...], w_ref[...], preferred_element_type=jnp.float32)


def _proj(hn, w_in_bf16, first_col_block, col_block_stride, n_groups, activation, out_dtype, name,
          groups_per_block=1):
    n_rows = hn.shape[0]
    tm = PROJ_ROWS // groups_per_block
    width = groups_per_block * D_MODEL
    n_work = (n_rows // tm) * n_groups
    cur = lambda s: jnp.minimum(s, n_work - 1)
    prev = lambda s: jnp.maximum(s - 1, 0)
    return pl.pallas_call(
        functools.partial(_proj_kernel, activation=activation, n_groups=n_groups),
        grid=(n_work + 1,),
        in_specs=[pl.BlockSpec((tm, D_MODEL), lambda s: (cur(s) // n_groups, 0)),
                  pl.BlockSpec((D_MODEL, width),
                               lambda s: (0, first_col_block + (cur(s) % n_groups) * col_block_stride))],
        out_specs=pl.BlockSpec((tm, D_MODEL), lambda s: (prev(s) // n_groups, prev(s) % n_groups)),
        out_shape=jax.ShapeDtypeStruct((n_rows, n_groups * D_MODEL), out_dtype),
        scratch_shapes=[pltpu.VMEM((tm, width), jnp.float32)],
        compiler_params=pltpu.CompilerParams(
            dimension_semantics=("arbitrary",),
            vmem_limit_bytes=V7X_VMEM_LIMIT_BYTES),
        name=name,
    )(hn, w_in_bf16)


def _act_qkv(raw, group):
    return raw * jnp.where(group == 0, Q_SCALE, 1.0).astype(jnp.float32)


def _act_silu(raw, group):
    return raw * _sigmoid(raw)


def _act_sigmoid(raw, group):
    return _sigmoid(raw)


def _act_glu(raw, group):
    return raw[:, :D_MODEL] * _sigmoid(raw[:, D_MODEL:])


def _softplus2_and_logsig2(l):
    g = jnp.log2(1.0 + jnp.exp2(jnp.minimum(l, -l)))
    sp = jnp.maximum(l, 0.0) + g
    return sp, l - sp


def _attention_kernel(q_ref, k_ref, v_ref, zs_ref, o_ref,
                      suffix_ref, qm_ref, vm_ref, acc_ref, rs_ref, z_ref, sp_ref, sp0_ref, own_ref):
    seq = q_ref.shape[0]
    t = ATT_TILE
    nh = ATT_HEADS
    n_blocks = seq // t
    n_diag_pairs = n_blocks // 2
    n_pairs = n_diag_pairs + n_blocks * (n_blocks - 1) // 4

    row = lax.broadcasted_iota(jnp.int32, (t, t), 0)
    col = lax.broadcasted_iota(jnp.int32, (t, t), 1)
    suffix_ref[...] = (row > col).astype(jnp.bfloat16)
    diag_mask = (col < row)[None]
    lane_head = lax.broadcasted_iota(jnp.int32, (t, nh * SB_HEAD_DIM), 1) // SB_HEAD_DIM

    def per_head(x):
        zero = jnp.zeros_like(x)
        return jnp.concatenate([jnp.where(lane_head == h, x, zero) for h in range(nh)], axis=0)

    def masked(x, fill):
        return jnp.where(diag_mask, x.reshape(nh, t, t), fill).reshape(nh * t, t)

    for j in range(n_blocks):
        qm_ref[j] = per_head(q_ref[j * t:(j + 1) * t, :])
        vm_ref[j] = per_head(v_ref[j * t:(j + 1) * t, :])
    acc_ref[...] = jnp.zeros_like(acc_ref)
    z_ref[0] = jnp.full_like(z_ref[0], MASKED_EXPONENT)

    def tile_of(p, c):
        n = 2 * (p - n_diag_pairs) + c
        qi = 1 + sum(jnp.where(n >= k * (k + 1) // 2, 1, 0) for k in range(1, n_blocks - 1))
        j = qi - 1 - (n - qi * (qi - 1) // 2)
        on_diag = p < n_diag_pairs
        return jnp.where(on_diag, 2 * p + c, qi), jnp.where(on_diag, 2 * p + c, j)

    def logits(p):
        ls = []
        for c in range(2):
            qi, j = tile_of(p, c)
            kt = k_ref[pl.ds(pl.multiple_of(j * t, t), t), :]
            ls.append(lax.dot_general(qm_ref[qi], kt, (((1,), (1,)), ((), ())),
                                      preferred_element_type=jnp.float32))
        return ls

    def later_sum(qi, sp_bf16, sp0, own, add_rs):
        later = jnp.dot(sp_bf16, suffix_ref[...], preferred_element_type=jnp.float32)
        tot = [later[:, :LANES], later[:, LANES:]]
        if add_rs:
            rs = rs_ref[qi]
            tot = [tot[0] + rs, tot[1] + rs]
        rs_ref[qi] = jnp.broadcast_to((tot[0] + sp0)[:, 0:1], (nh * t, LANES))
        return own - jnp.concatenate(tot, axis=1)

    def diag_exponent_stage(p, ls):
        for c in range(2):
            qi, _ = tile_of(p, c)
            sp, own = _softplus2_and_logsig2(ls[c])
            sp = masked(sp, 0.0)
            z = later_sum(qi, sp.astype(jnp.bfloat16), sp[:, :LANES], own, False)
            z_ref[0, c] = masked(z, MASKED_EXPONENT)

    def weight_stage(p, slot):
        for c in range(2):
            qi, j = tile_of(p, c)
            wb = jnp.exp2(z_ref[slot, c]).astype(jnp.bfloat16)
            w_cat = jnp.concatenate([wb[h * t:(h + 1) * t] for h in range(nh)], axis=1)
            acc_ref[qi] += jnp.dot(w_cat, vm_ref[j], preferred_element_type=jnp.float32)

    def softplus_stage(p, slot):
        for c, l in enumerate(logits(p)):
            sp, own = _softplus2_and_logsig2(l)
            sp_ref[slot, c] = sp.astype(jnp.bfloat16)
            sp0_ref[slot, c] = sp[:, :LANES]
            own_ref[slot, c] = own

    def later_sum_stage(p, slot):
        for c in range(2):
            qi, _ = tile_of(p, c)
            z_ref[slot, c] = later_sum(qi, sp_ref[slot, c], sp0_ref[slot, c], own_ref[slot, c], True)

    def diag_pair(p, _):
        ls = logits(p)
        weight_stage(jnp.maximum(p - 1, 0), 0)
        diag_exponent_stage(p, ls)
        return 0

    first = n_diag_pairs

    def two_full_pairs(i, _):
        p = first + 2 + 2 * i
        weight_stage(p - 2, 0)
        softplus_stage(p, 0)
        later_sum_stage(p - 1, 1)
        weight_stage(p - 1, 1)
        softplus_stage(p + 1, 1)
        later_sum_stage(p, 0)
        return 0

    lax.fori_loop(0, first, diag_pair, 0)
    weight_stage(first - 1, 0)
    softplus_stage(first, 0)
    softplus_stage(first + 1, 1)
    later_sum_stage(first, 0)
    lax.fori_loop(0, (n_pairs - first - 2) // 2, two_full_pairs, 0)
    later_sum_stage(n_pairs - 1, 1)
    weight_stage(n_pairs - 2, 0)
    weight_stage(n_pairs - 1, 1)

    for qi in range(n_blocks):
        gate = zs_ref[qi * t:(qi + 1) * t, :].astype(jnp.float32)
        o_ref[qi * t:(qi + 1) * t, :] = (acc_ref[qi] * gate).astype(o_ref.dtype)


def _attention(qkv3, zz3):
    bsz, seq, _ = qkv3.shape
    lanes = ATT_HEADS * SB_HEAD_DIM
    per_group = D_MODEL // lanes
    n_blocks = seq // ATT_TILE
    stack = ATT_HEADS * ATT_TILE
    assert n_blocks % 4 == 0
    group_spec = lambda g: pl.BlockSpec((None, seq, lanes), lambda b, p: (b, 0, g * per_group + p))
    return pl.pallas_call(
        _attention_kernel,
        grid=(bsz, per_group),
        in_specs=[group_spec(0), group_spec(1), group_spec(2), group_spec(0)],
        out_specs=pl.BlockSpec((None, seq, lanes), lambda b, p: (b, 0, p)),
        out_shape=jax.ShapeDtypeStruct((bsz, seq, D_MODEL), jnp.bfloat16),
        scratch_shapes=[pltpu.VMEM((ATT_TILE, ATT_TILE), jnp.bfloat16),
                        pltpu.VMEM((n_blocks, stack, lanes), jnp.bfloat16),
                        pltpu.VMEM((n_blocks, stack, lanes), jnp.bfloat16),
                        pltpu.VMEM((n_blocks, ATT_TILE, lanes), jnp.float32),
                        pltpu.VMEM((n_blocks, stack, LANES), jnp.float32),
                        pltpu.VMEM((2, 2, stack, ATT_TILE), jnp.float32),
                        pltpu.VMEM((2, 2, stack, ATT_TILE), jnp.bfloat16),
                        pltpu.VMEM((2, 2, stack, LANES), jnp.float32),
                        pltpu.VMEM((2, 2, stack, ATT_TILE), jnp.float32)],
        compiler_params=pltpu.CompilerParams(
            dimension_semantics=("arbitrary", "arbitrary"),
            vmem_limit_bytes=V7X_VMEM_LIMIT_BYTES),
        name="sb_attention",
    )(qkv3, qkv3, qkv3, zz3)


def _out_kernel(x_ref, a_ref, u_ref, halo_ref, zc_ref, gs_ref, gc_ref,
                ln_in_g_ref, ln_in_b_ref, cw_ref, cb_ref, cg_ref, cbeta_ref,
                w_sb_ref, w_cv_ref, w_out_ref, ln_g_ref, ln_b_ref,
                o_ref, ext_ref, conv_ref, wb_ref, *, tiles_per_seq):
    i = pl.program_id(0)
    tm = x_ref.shape[0]
    n_slabs = D_MODEL // LANES

    @pl.when(i == 0)
    def _():
        for tap in range(CV_KERNEL):
            for s in range(n_slabs):
                wb_ref[s, tap * SUBLANES:(tap + 1) * SUBLANES, :] = jnp.broadcast_to(
                    cw_ref[tap:tap + 1, s * LANES:(s + 1) * LANES], (SUBLANES, LANES))

    first = (i % tiles_per_seq) == 0
    for s in range(n_slabs):
        halo = halo_ref[:, s * LANES:(s + 1) * LANES]
        ext_ref[s, 0:CONV_HALO, :] = jnp.where(first, jnp.zeros_like(halo), halo)
        ext_ref[s, CONV_HALO:, :] = u_ref[:, s * LANES:(s + 1) * LANES]

    base = CONV_HALO - (CV_KERNEL - 1)

    def conv_chunk(c, _):
        r0 = pl.multiple_of(c * CONV_CHUNK, CONV_CHUNK)
        for s in range(n_slabs):
            win = [ext_ref[s, pl.ds(r0 + base + m, SUBLANES, stride=CONV_STRIDE), :]
                   for m in range(CONV_STRIDE + CV_KERNEL - 1)]
            acc = [jnp.zeros((SUBLANES, LANES), jnp.float32) for _ in range(CONV_STRIDE)]
            for tap in range(CV_KERNEL):
                w = wb_ref[s, tap * SUBLANES:(tap + 1) * SUBLANES, :]
                for ph in range(CONV_STRIDE):
                    acc[ph] = acc[ph] + w * win[ph + tap]
            for ph in range(CONV_STRIDE):
                conv_ref[s, pl.ds(r0 + ph, SUBLANES, stride=CONV_STRIDE), :] = acc[ph]
        return 0

    lax.fori_loop(0, tm // CONV_CHUNK, conv_chunk, 0)

    conv = jnp.concatenate([conv_ref[s] for s in range(n_slabs)], axis=-1) + cb_ref[...]
    cn = _layer_norm_f32(conv, cg_ref[...], cbeta_ref[...])
    ucv = cn * _sigmoid(cn) * zc_ref[...].astype(jnp.float32)
    y_cv = jnp.dot(ucv.astype(jnp.bfloat16), w_cv_ref[...], preferred_element_type=jnp.float32)
    y_sb = jnp.dot(a_ref[...], w_sb_ref[...], preferred_element_type=jnp.float32)
    merged = (gs_ref[...].astype(jnp.float32) * y_sb + gc_ref[...].astype(jnp.float32) * y_cv)
    proj = jnp.dot(merged.astype(jnp.bfloat16), w_out_ref[...], preferred_element_type=jnp.float32)
    h0 = _layer_norm_f32(x_ref[...], ln_in_g_ref[...], ln_in_b_ref[...])
    o_ref[...] = _layer_norm_f32(DEEPNORM_ALPHA * h0 + proj, ln_g_ref[...], ln_b_ref[...])


def _out_stage(x2d, a, u, zz, gg, ln_in_g, ln_in_b, conv_w, conv_b, conv_ln_g, conv_ln_b,
               w_sb, w_cv, w_out, ln_g, ln_b, seq):
    n_rows = x2d.shape[0]
    tm = OUT_ROWS
    halo_per_tile = tm // CONV_HALO
    row_spec = pl.BlockSpec((tm, D_MODEL), lambda i: (i, 0))
    halo_spec = pl.BlockSpec((CONV_HALO, D_MODEL),
                             lambda i: (jnp.maximum(i * halo_per_tile - 1, 0), 0))
    act_spec = lambda g: pl.BlockSpec((tm, D_MODEL), lambda i: (i, g))
    vec_spec = pl.BlockSpec((1, D_MODEL), lambda i: (0, 0))
    mat_spec = pl.BlockSpec((D_MODEL, D_MODEL), lambda i: (0, 0))
    return pl.pallas_call(
        functools.partial(_out_kernel, tiles_per_seq=seq // tm),
        grid=(n_rows // tm,),
        in_specs=[row_spec, row_spec, row_spec, halo_spec,
                  act_spec(1), act_spec(0), act_spec(1),
                  vec_spec, vec_spec,
                  pl.BlockSpec((CV_KERNEL, D_MODEL), lambda i: (0, 0)),
                  vec_spec, vec_spec, vec_spec,
                  mat_spec, mat_spec, mat_spec, vec_spec, vec_spec],
        out_specs=row_spec,
        out_shape=jax.ShapeDtypeStruct((n_rows, D_MODEL), jnp.float32),
        scratch_shapes=[pltpu.VMEM((D_MODEL // LANES, tm + CONV_HALO, LANES), jnp.float32),
                        pltpu.VMEM((D_MODEL // LANES, tm, LANES), jnp.float32),
                        pltpu.VMEM((D_MODEL // LANES, CV_KERNEL * SUBLANES, LANES), jnp.float32)],
        compiler_params=pltpu.CompilerParams(
            dimension_semantics=("arbitrary",),
            vmem_limit_bytes=V7X_VMEM_LIMIT_BYTES),
        name="out_stage",
    )(x2d, a, u, u, zz, gg, gg, ln_in_g, ln_in_b, conv_w, conv_b, conv_ln_g, conv_ln_b,
      w_sb, w_cv, w_out, ln_g, ln_b)


def kernel(x, ln_in_g, ln_in_b, w_in, w_sb_proj, conv_w, conv_b, conv_ln_g, conv_ln_b,
           w_cv_proj, w_out, ln_post_g, ln_post_b):
    bsz, seq, d = x.shape
    assert d == D_MODEL and w_in.shape == (DEPTH, D_MODEL, N_SPLITS * D_MODEL)
    assert seq % ATT_TILE == 0 and seq % OUT_ROWS == 0 and (bsz * seq) % PROJ_ROWS == 0
    x2d = x.reshape(bsz * seq, d)
    row = lambda p: p.reshape(1, D_MODEL)
    bf16 = lambda w: w.astype(jnp.bfloat16)

    w_in_bf16 = bf16(w_in[0])
    hn = _ln_in(x2d, row(ln_in_g), row(ln_in_b))
    qkv = _proj(hn, w_in_bf16, COL_Q, 1, 3, _act_qkv, jnp.bfloat16, "qkv_proj")
    zz = _proj(hn, w_in_bf16, COL_ZS, COL_ZC - COL_ZS, 2, _act_silu, jnp.bfloat16, "silu_proj")
    gg = _proj(hn, w_in_bf16, COL_GS, 1, 2, _act_sigmoid, jnp.bfloat16, "gate_proj")
    u = _proj(hn, w_in_bf16, COL_GLU_VALUE // 2, 1, 1, _act_glu, jnp.float32, "glu_proj", groups_per_block=2)
    a = _attention(qkv.reshape(bsz, seq, 3 * D_MODEL), zz.reshape(bsz, seq, 2 * D_MODEL))
    out = _out_stage(x2d, a.reshape(bsz * seq, D_MODEL), u, zz, gg, row(ln_in_g), row(ln_in_b),
                     conv_w[0], row(conv_b[0]), row(conv_ln_g[0]), row(conv_ln_b[0]),
                     bf16(w_sb_proj[0]), bf16(w_cv_proj[0]), bf16(w_out[0]),
                     row(ln_post_g[0]), row(ln_post_b[0]), seq)
    return out.reshape(bsz, seq, d)
```

```python
import functools
import math

import jax
import jax.numpy as jnp
from jax import lax
from jax.experimental import pallas as pl
from jax.experimental.pallas import tpu as pltpu

D_MODEL = 1024
SB_HEADS = 16
SB_HEAD_DIM = 64
CV_KERNEL = 31
LN_EPS = 1e-5
DEPTH = 1
DEEPNORM_ALPHA = (2.0 * DEPTH) ** 0.25
N_SPLITS = 9
COL_Q, COL_K, COL_V, COL_ZS, COL_GLU_VALUE, COL_GLU_GATE, COL_ZC, COL_GS, COL_GC = range(N_SPLITS)

LOG2E = 1.4426950408889634
Q_SCALE = LOG2E / math.sqrt(SB_HEAD_DIM)

V7X_VMEM_LIMIT_BYTES = 56 * 1024 * 1024
LANES = 128
SUBLANES = 8

PROJ_ROWS = 2048
ATT_TILE = 256
ATT_HEADS = 4
MASKED_EXPONENT = -1e30
ATT_PAIRS_PER_ITER = 6
OUT_ROWS = 512
CONV_HALO = 32
CONV_STRIDE = 4
CONV_CHUNK = CONV_STRIDE * SUBLANES


def _layer_norm_f32(x, g, b):
    mu = jnp.mean(x, axis=-1, keepdims=True)
    xc = x - mu
    var = jnp.mean(xc * xc, axis=-1, keepdims=True)
    return xc * lax.rsqrt(var + LN_EPS) * g + b


def _sigmoid(x):
    return 1.0 / (1.0 + jnp.exp2(x * (-LOG2E)))


def _ln_kernel(x_ref, g_ref, b_ref, hn_ref):
    hn_ref[...] = _layer_norm_f32(x_ref[...], g_ref[...], b_ref[...]).astype(hn_ref.dtype)


def _ln_in(x2d, ln_g, ln_b):
    n_rows = x2d.shape[0]
    tm = PROJ_ROWS
    return pl.pallas_call(
        _ln_kernel,
        grid=(n_rows // tm,),
        in_specs=[pl.BlockSpec((tm, D_MODEL), lambda i: (i, 0)),
                  pl.BlockSpec((1, D_MODEL), lambda i: (0, 0)),
                  pl.BlockSpec((1, D_MODEL), lambda i: (0, 0))],
        out_specs=pl.BlockSpec((tm, D_MODEL), lambda i: (i, 0)),
        out_shape=jax.ShapeDtypeStruct((n_rows, D_MODEL), jnp.bfloat16),
        compiler_params=pltpu.CompilerParams(
            dimension_semantics=("arbitrary",),
            vmem_limit_bytes=V7X_VMEM_LIMIT_BYTES),
        name="ln_in",
    )(x2d, ln_g, ln_b)


def _proj_kernel(hn_ref, w_ref, out_ref, raw_ref, *, activation, n_groups):
    s = pl.program_id(0)

    @pl.when(s == 0)
    def _():
        raw_ref[...] = jnp.zeros_like(raw_ref)

    group = jnp.maximum(s - 1, 0) % n_groups
    out_ref[...] = activation(raw_ref[...], group).astype(out_ref.dtype)
    raw_ref[...] = jnp.dot(hn_ref[...], w_ref[...], preferred_element_type=jnp.float32)


def _proj(hn, w_in_bf16, first_col_block, col_block_stride, n_groups, activation, out_dtype, name,
          groups_per_block=1):
    n_rows = hn.shape[0]
    tm = PROJ_ROWS // groups_per_block
    width = groups_per_block * D_MODEL
    n_work = (n_rows // tm) * n_groups
    cur = lambda s: jnp.minimum(s, n_work - 1)
    prev = lambda s: jnp.maximum(s - 1, 0)
    return pl.pallas_call(
        functools.partial(_proj_kernel, activation=activation, n_groups=n_groups),
        grid=(n_work + 1,),
        in_specs=[pl.BlockSpec((tm, D_MODEL), lambda s: (cur(s) // n_groups, 0)),
                  pl.BlockSpec((D_MODEL, width),
                               lambda s: (0, first_col_block + (cur(s) % n_groups) * col_block_stride))],
        out_specs=pl.BlockSpec((tm, D_MODEL), lambda s: (prev(s) // n_groups, prev(s) % n_groups)),
        out_shape=jax.ShapeDtypeStruct((n_rows, n_groups * D_MODEL), out_dtype),
        scratch_shapes=[pltpu.VMEM((tm, width), jnp.float32)],
        compiler_params=pltpu.CompilerParams(
            dimension_semantics=("arbitrary",),
            vmem_limit_bytes=V7X_VMEM_LIMIT_BYTES),
        name=name,
    )(hn, w_in_bf16)


def _act_qkv(raw, group):
    return raw * jnp.where(group == 0, Q_SCALE, 1.0).astype(jnp.float32)


def _act_silu(raw, group):
    return raw * _sigmoid(raw)


def _act_sigmoid(raw, group):
    return _sigmoid(raw)


def _act_glu(raw, group):
    return raw[:, :D_MODEL] * _sigmoid(raw[:, D_MODEL:])


def _softplus2_and_logsig2(l):
    g = jnp.log2(1.0 + jnp.exp2(jnp.minimum(l, -l)))
    sp = jnp.maximum(l, 0.0) + g
    return sp, l - sp


def _attention_kernel(q_ref, k_ref, v_ref, zs_ref, o_ref,
                      suffix_ref, qm_ref, vm_ref, acc_ref, rs_ref, z_ref, sp_ref, sp0_ref, own_ref):
    seq = q_ref.shape[0]
    t = ATT_TILE
    nh = ATT_HEADS
    n_blocks = seq // t
    n_diag_pairs = n_blocks // 2
    n_pairs = n_diag_pairs + n_blocks * (n_blocks - 1) // 4

    row = lax.broadcasted_iota(jnp.int32, (t, t), 0)
    col = lax.broadcasted_iota(jnp.int32, (t, t), 1)
    suffix_ref[...] = (row > col).astype(jnp.bfloat16)
    diag_mask = (col < row)[None]
    lane_head = lax.broadcasted_iota(jnp.int32, (t, nh * SB_HEAD_DIM), 1) // SB_HEAD_DIM

    def per_head(x):
        zero = jnp.zeros_like(x)
        return jnp.concatenate([jnp.where(lane_head == h, x, zero) for h in range(nh)], axis=0)

    def masked(x, fill):
        return jnp.where(diag_mask, x.reshape(nh, t, t), fill).reshape(nh * t, t)

    for j in range(n_blocks):
        qm_ref[j] = per_head(q_ref[j * t:(j + 1) * t, :])
        vm_ref[j] = per_head(v_ref[j * t:(j + 1) * t, :])
    acc_ref[...] = jnp.zeros_like(acc_ref)
    z_ref[0] = jnp.full_like(z_ref[0], MASKED_EXPONENT)

    def tile_of(p, c):
        n = 2 * (p - n_diag_pairs) + c
        qi = 1 + sum(jnp.where(n >= k * (k + 1) // 2, 1, 0) for k in range(1, n_blocks - 1))
        j = qi - 1 - (n - qi * (qi - 1) // 2)
        on_diag = p < n_diag_pairs
        return jnp.where(on_diag, 2 * p + c, qi), jnp.where(on_diag, 2 * p + c, j)

    def logits(p):
        ls = []
        for c in range(2):
            qi, j = tile_of(p, c)
            kt = k_ref[pl.ds(pl.multiple_of(j * t, t), t), :]
            ls.append(lax.dot_general(qm_ref[qi], kt, (((1,), (1,)), ((), ())),
                                      preferred_element_type=jnp.float32))
        return ls

    def later_sum(qi, sp_bf16, sp0, own, add_rs):
        later = jnp.dot(sp_bf16, suffix_ref[...], preferred_element_type=jnp.float32)
        tot = [later[:, :LANES], later[:, LANES:]]
        if add_rs:
            rs = rs_ref[qi]
            tot = [tot[0] + rs, tot[1] + rs]
        rs_ref[qi] = jnp.broadcast_to((tot[0] + sp0)[:, 0:1], (nh * t, LANES))
        return own - jnp.concatenate(tot, axis=1)

    def diag_exponent_stage(p, ls):
        for c in range(2):
            qi, _ = tile_of(p, c)
            sp, own = _softplus2_and_logsig2(ls[c])
            sp = masked(sp, 0.0)
            z = later_sum(qi, sp.astype(jnp.bfloat16), sp[:, :LANES], own, False)
            z_ref[0, c] = masked(z, MASKED_EXPONENT)

    def weight_stage(p, slot):
        for c in range(2):
            qi, j = tile_of(p, c)
            wb = jnp.exp2(z_ref[slot, c]).astype(jnp.bfloat16)
            w_cat = jnp.concatenate([wb[h * t:(h + 1) * t] for h in range(nh)], axis=1)
            acc_ref[qi] += jnp.dot(w_cat, vm_ref[j], preferred_element_type=jnp.float32)

    def softplus_stage(p, slot):
        for c, l in enumerate(logits(p)):
            sp, own = _softplus2_and_logsig2(l)
            sp_ref[slot, c] = sp.astype(jnp.bfloat16)
            sp0_ref[slot, c] = sp[:, :LANES]
            own_ref[slot, c] = own

    def later_sum_stage(p, slot):
        for c in range(2):
            qi, _ = tile_of(p, c)
            z_ref[slot, c] = later_sum(qi, sp_ref[slot, c], sp0_ref[slot, c], own_ref[slot, c], True)

    def diag_pair(p, _):
        ls = logits(p)
        weight_stage(jnp.maximum(p - 1, 0), 0)
        diag_exponent_stage(p, ls)
        return 0

    first = n_diag_pairs

    def full_pairs(i, _):
        for k in range(0, ATT_PAIRS_PER_ITER, 2):
            p = first + 2 + ATT_PAIRS_PER_ITER * i + k
            weight_stage(p - 2, 0)
            softplus_stage(p, 0)
            later_sum_stage(p - 1, 1)
            weight_stage(p - 1, 1)
            softplus_stage(p + 1, 1)
            later_sum_stage(p, 0)
        return 0

    lax.fori_loop(0, first, diag_pair, 0)
    weight_stage(first - 1, 0)
    softplus_stage(first, 0)
    softplus_stage(first + 1, 1)
    later_sum_stage(first, 0)
    lax.fori_loop(0, (n_pairs - first - 2) // ATT_PAIRS_PER_ITER, full_pairs, 0)
    later_sum_stage(n_pairs - 1, 1)
    weight_stage(n_pairs - 2, 0)
    weight_stage(n_pairs - 1, 1)

    for qi in range(n_blocks):
        gate = zs_ref[qi * t:(qi + 1) * t, :].astype(jnp.float32)
        o_ref[qi * t:(qi + 1) * t, :] = (acc_ref[qi] * gate).astype(o_ref.dtype)


def _attention(qkv3, zz3):
    bsz, seq, _ = qkv3.shape
    lanes = ATT_HEADS * SB_HEAD_DIM
    per_group = D_MODEL // lanes
    n_blocks = seq // ATT_TILE
    stack = ATT_HEADS * ATT_TILE
    assert n_blocks % 4 == 0
    assert (n_blocks * (n_blocks - 1) // 4 - 2) % ATT_PAIRS_PER_ITER == 0
    group_spec = lambda g: pl.BlockSpec((None, seq, lanes), lambda b, p: (b, 0, g * per_group + p))
    return pl.pallas_call(
        _attention_kernel,
        grid=(bsz, per_group),
        in_specs=[group_spec(0), group_spec(1), group_spec(2), group_spec(0)],
        out_specs=pl.BlockSpec((None, seq, lanes), lambda b, p: (b, 0, p)),
        out_shape=jax.ShapeDtypeStruct((bsz, seq, D_MODEL), jnp.bfloat16),
        scratch_shapes=[pltpu.VMEM((ATT_TILE, ATT_TILE), jnp.bfloat16),
                        pltpu.VMEM((n_blocks, stack, lanes), jnp.bfloat16),
                        pltpu.VMEM((n_blocks, stack, lanes), jnp.bfloat16),
                        pltpu.VMEM((n_blocks, ATT_TILE, lanes), jnp.float32),
                        pltpu.VMEM((n_blocks, stack, LANES), jnp.float32),
                        pltpu.VMEM((2, 2, stack, ATT_TILE), jnp.float32),
                        pltpu.VMEM((2, 2, stack, ATT_TILE), jnp.bfloat16),
                        pltpu.VMEM((2, 2, stack, LANES), jnp.float32),
                        pltpu.VMEM((2, 2, stack, ATT_TILE), jnp.float32)],
        compiler_params=pltpu.CompilerParams(
            dimension_semantics=("arbitrary", "arbitrary"),
            vmem_limit_bytes=V7X_VMEM_LIMIT_BYTES),
        name="sb_attention",
    )(qkv3, qkv3, qkv3, zz3)


def _out_kernel(x_ref, a_ref, u_ref, halo_ref, zc_ref, gs_ref, gc_ref,
                ln_in_g_ref, ln_in_b_ref, cw_ref, cb_ref, cg_ref, cbeta_ref,
                w_sb_ref, w_cv_ref, w_out_ref, ln_g_ref, ln_b_ref,
                o_ref, ext_ref, conv_ref, wb_ref, *, tiles_per_seq):
    i = pl.program_id(0)
    tm = x_ref.shape[0]
    n_slabs = D_MODEL // LANES

    @pl.when(i == 0)
    def _():
        for tap in range(CV_KERNEL):
            for s in range(n_slabs):
                wb_ref[s, tap * SUBLANES:(tap + 1) * SUBLANES, :] = jnp.broadcast_to(
                    cw_ref[tap:tap + 1, s * LANES:(s + 1) * LANES], (SUBLANES, LANES))

    first = (i % tiles_per_seq) == 0
    for s in range(n_slabs):
        halo = halo_ref[:, s * LANES:(s + 1) * LANES]
        ext_ref[s, 0:CONV_HALO, :] = jnp.where(first, jnp.zeros_like(halo), halo)
        ext_ref[s, CONV_HALO:, :] = u_ref[:, s * LANES:(s + 1) * LANES]

    base = CONV_HALO - (CV_KERNEL - 1)

    def conv_chunk(c, _):
        r0 = pl.multiple_of(c * CONV_CHUNK, CONV_CHUNK)
        for s in range(n_slabs):
            win = [ext_ref[s, pl.ds(r0 + base + m, SUBLANES, stride=CONV_STRIDE), :]
                   for m in range(CONV_STRIDE + CV_KERNEL - 1)]
            acc = [jnp.zeros((SUBLANES, LANES), jnp.float32) for _ in range(CONV_STRIDE)]
            for tap in range(CV_KERNEL):
                w = wb_ref[s, tap * SUBLANES:(tap + 1) * SUBLANES, :]
                for ph in range(CONV_STRIDE):
                    acc[ph] = acc[ph] + w * win[ph + tap]
            for ph in range(CONV_STRIDE):
                conv_ref[s, pl.ds(r0 + ph, SUBLANES, stride=CONV_STRIDE), :] = acc[ph]
        return 0

    lax.fori_loop(0, tm // CONV_CHUNK, conv_chunk, 0)

    conv = jnp.concatenate([conv_ref[s] for s in range(n_slabs)], axis=-1) + cb_ref[...]
    cn = _layer_norm_f32(conv, cg_ref[...], cbeta_ref[...])
    ucv = cn * _sigmoid(cn) * zc_ref[...].astype(jnp.float32)
    y_cv = jnp.dot(ucv.astype(jnp.bfloat16), w_cv_ref[...], preferred_element_type=jnp.float32)
    y_sb = jnp.dot(a_ref[...], w_sb_ref[...], preferred_element_type=jnp.float32)
    merged = (gs_ref[...].astype(jnp.float32) * y_sb + gc_ref[...].astype(jnp.float32) * y_cv)
    proj = jnp.dot(merged.astype(jnp.bfloat16), w_out_ref[...], preferred_element_type=jnp.float32)
    h0 = _layer_norm_f32(x_ref[...], ln_in_g_ref[...], ln_in_b_ref[...])
    o_ref[...] = _layer_norm_f32(DEEPNORM_ALPHA * h0 + proj, ln_g_ref[...], ln_b_ref[...])


def _out_stage(x2d, a, u, zz, gg, ln_in_g, ln_in_b, conv_w, conv_b, conv_ln_g, conv_ln_b,
               w_sb, w_cv, w_out, ln_g, ln_b, seq):
    n_rows = x2d.shape[0]
    tm = OUT_ROWS
    halo_per_tile = tm // CONV_HALO
    row_spec = pl.BlockSpec((tm, D_MODEL), lambda i: (i, 0))
    halo_spec = pl.BlockSpec((CONV_HALO, D_MODEL),
                             lambda i: (jnp.maximum(i * halo_per_tile - 1, 0), 0))
    act_spec = lambda g: pl.BlockSpec((tm, D_MODEL), lambda i: (i, g))
    vec_spec = pl.BlockSpec((1, D_MODEL), lambda i: (0, 0))
    mat_spec = pl.BlockSpec((D_MODEL, D_MODEL), lambda i: (0, 0))
    return pl.pallas_call(
        functools.partial(_out_kernel, tiles_per_seq=seq // tm),
        grid=(n_rows // tm,),
        in_specs=[row_spec, row_spec, row_spec, halo_spec,
                  act_spec(1), act_spec(0), act_spec(1),
                  vec_spec, vec_spec,
                  pl.BlockSpec((CV_KERNEL, D_MODEL), lambda i: (0, 0)),
                  vec_spec, vec_spec, vec_spec,
                  mat_spec, mat_spec, mat_spec, vec_spec, vec_spec],
        out_specs=row_spec,
        out_shape=jax.ShapeDtypeStruct((n_rows, D_MODEL), jnp.float32),
        scratch_shapes=[pltpu.VMEM((D_MODEL // LANES, tm + CONV_HALO, LANES), jnp.float32),
                        pltpu.VMEM((D_MODEL // LANES, tm, LANES), jnp.float32),
                        pltpu.VMEM((D_MODEL // LANES, CV_KERNEL * SUBLANES, LANES), jnp.float32)],
        compiler_params=pltpu.CompilerParams(
            dimension_semantics=("arbitrary",),
            vmem_limit_bytes=V7X_VMEM_LIMIT_BYTES),
        name="out_stage",
    )(x2d, a, u, u, zz, gg, gg, ln_in_g, ln_in_b, conv_w, conv_b, conv_ln_g, conv_ln_b,
      w_sb, w_cv, w_out, ln_g, ln_b)


def kernel(x, ln_in_g, ln_in_b, w_in, w_sb_proj, conv_w, conv_b, conv_ln_g, conv_ln_b,
           w_cv_proj, w_out, ln_post_g, ln_post_b):
    bsz, seq, d = x.shape
    assert d == D_MODEL and w_in.shape == (DEPTH, D_MODEL, N_SPLITS * D_MODEL)
    assert seq % ATT_TILE == 0 and seq % OUT_ROWS == 0 and (bsz * seq) % PROJ_ROWS == 0
    x2d = x.reshape(bsz * seq, d)
    row = lambda p: p.reshape(1, D_MODEL)
    bf16 = lambda w: w.astype(jnp.bfloat16)

    w_in_bf16 = bf16(w_in[0])
    hn = _ln_in(x2d, row(ln_in_g), row(ln_in_b))
    qkv = _proj(hn, w_in_bf16, COL_Q, 1, 3, _act_qkv, jnp.bfloat16, "qkv_proj")
    zz = _proj(hn, w_in_bf16, COL_ZS, COL_ZC - COL_ZS, 2, _act_silu, jnp.bfloat16, "silu_proj")
    gg = _proj(hn, w_in_bf16, COL_GS, 1, 2, _act_sigmoid, jnp.bfloat16, "gate_proj")
    u = _proj(hn, w_in_bf16, COL_GLU_VALUE // 2, 1, 1, _act_glu, jnp.float32, "glu_proj", groups_per_block=2)
    a = _attention(qkv.reshape(bsz, seq, 3 * D_MODEL), zz.reshape(bsz, seq, 2 * D_MODEL))
    out = _out_stage(x2d, a.reshape(bsz * seq, D_MODEL), u, zz, gg, row(ln_in_g), row(ln_in_b),
                     conv_w[0], row(conv_b[0]), row(conv_ln_g[0]), row(conv_ln_b[0]),
                     bf16(w_sb_proj[0]), bf16(w_cv_proj[0]), bf16(w_out[0]),
                     row(ln_post_g[0]), row(ln_post_b[0]), seq)
    return out.reshape(bsz, seq, d)
```
